```python
import math, functools
import jax, jax.numpy as jnp
from jax import lax
import numpy as np

D_MODEL = 1024
BATCH = 4
SEQ = 4096
DEPTH = 1
DEC_BATCH = 128
DEC_SEQ = 8
PAST_LEN = 2048
PAGE_SIZE = 128

SB_HEADS = 8
SB_HEAD_DIM = 64
SB_WIDTH = SB_HEADS * SB_HEAD_DIM
SB_BIAS_INIT = -6.0
SSM_GROUP = 16
SSM_WIDTH = D_MODEL // 2
SSM_GROUPS = SSM_WIDTH // SSM_GROUP
SSM_STATE = 64
D_FF = 2816
Q_BLOCK = 128
ADA_CHUNKS = 9
RMS_EPS = 1e-6
DT_MIN = 1e-3
DT_MAX = 1e-1
IN_SPLITS = (SB_WIDTH, 2 * SB_WIDTH, 3 * SB_WIDTH, 3 * SB_WIDTH + SSM_WIDTH,
             3 * SB_WIDTH + SSM_WIDTH + D_MODEL)
IN_WIDTH = 3 * SB_WIDTH + SSM_WIDTH + 2 * D_MODEL

kernel_name = 'stickbreak_s5_macaron_adaln_step'


def rmsnorm(x, gain):
    xf = x.astype(jnp.float32)
    n = xf * lax.rsqrt(jnp.mean(xf * xf, axis=-1, keepdims=True) + RMS_EPS)
    return (n * gain.astype(jnp.float32)).astype(x.dtype)


def rms_modulate(x, gain, shift, scale):
    xf = x.astype(jnp.float32)
    n = xf * lax.rsqrt(jnp.mean(xf * xf, axis=-1, keepdims=True) + RMS_EPS) * gain.astype(jnp.float32)
    n = n * (1.0 + scale[:, None, :].astype(jnp.float32)) + shift[:, None, :].astype(jnp.float32)
    return n.astype(x.dtype)


def swiglu(u, w_in, w_out):
    g, up = jnp.split(u @ w_in, 2, axis=-1)
    return (jax.nn.silu(g) * up) @ w_out


def stick_breaking(q, k, v, bias, q_pos, k_pos):
    z = (jnp.einsum('bqhd,bkhd->bhqk', q, k).astype(jnp.float32) * (SB_HEAD_DIM ** -0.5)
         + bias.astype(jnp.float32)[None, :, None, None])
    mask = k_pos[None, :] < q_pos[:, None]
    log_beta = jax.nn.log_sigmoid(z)
    log_keep = jnp.where(mask, log_beta - z, 0.0)
    after = lax.cumsum(log_keep, axis=3, reverse=True) - log_keep
    w = jnp.where(mask, jnp.exp(log_beta + after), 0.0)
    return jnp.einsum('bhqk,bkhd->bqhd', w.astype(v.dtype), v)


def sb_prompt(q, k, v, bias):
    b, s, h, d = q.shape
    nblk = s // Q_BLOCK
    qb = q.reshape(b, nblk, Q_BLOCK, h, d).transpose(1, 0, 2, 3, 4)
    k_pos = jnp.arange(s)

    def one_block(args):
        q_blk, i = args
        q_pos = i * Q_BLOCK + jnp.arange(Q_BLOCK)
        return stick_breaking(q_blk, k, v, bias, q_pos, k_pos)

    out = lax.map(one_block, (qb, jnp.arange(nblk)))
    return out.transpose(1, 0, 2, 3, 4).reshape(b, s, h, d)


def sb_sample(q, k, v, bias, past_k, past_v):
    past = past_k.shape[1]
    t = q.shape[1]
    k_all = jnp.concatenate([past_k, k], axis=1)
    v_all = jnp.concatenate([past_v, v], axis=1)
    return stick_breaking(q, k_all, v_all, bias, past + jnp.arange(t), jnp.arange(past + t))


def s5_ssm(u, lam_re, lam_im, log_dt, b_re, b_im, c_re, c_im, d_skip, s0_re, s0_im):
    f32 = jnp.float32
    bsz, L, _ = u.shape
    uf = u.astype(f32)
    ug = uf.reshape(bsz, L, SSM_GROUPS, SSM_GROUP)
    lr, li = lam_re.astype(f32), lam_im.astype(f32)
    dt = jnp.exp(log_dt.astype(f32))[:, None]
    mag = jnp.exp(lr * dt)
    ab_re, ab_im = mag * jnp.cos(li * dt), mag * jnp.sin(li * dt)
    den = lr * lr + li * li
    nr, ni = ab_re - 1.0, ab_im
    zr, zi = (nr * lr + ni * li) / den, (ni * lr - nr * li) / den
    br, bi = b_re.astype(f32), b_im.astype(f32)
    bb_re = zr[..., None] * br - zi[..., None] * bi
    bb_im = zr[..., None] * bi + zi[..., None] * br
    bu_re = jnp.einsum('blgh,gph->blgp', ug, bb_re)
    bu_im = jnp.einsum('blgh,gph->blgp', ug, bb_im)
    s0r, s0i = s0_re.astype(f32), s0_im.astype(f32)
    bu_re = bu_re.at[:, 0].add(ab_re * s0r - ab_im * s0i)
    bu_im = bu_im.at[:, 0].add(ab_re * s0i + ab_im * s0r)
    a_re = jnp.broadcast_to(ab_re, bu_re.shape)
    a_im = jnp.broadcast_to(ab_im, bu_im.shape)

    def combine(e1, e2):
        a1r, a1i, b1r, b1i = e1
        a2r, a2i, b2r, b2i = e2
        return (a1r * a2r - a1i * a2i,
                a1r * a2i + a1i * a2r,
                a2r * b1r - a2i * b1i + b2r,
                a2r * b1i + a2i * b1r + b2i)

    _, _, x_re, x_im = lax.associative_scan(combine, (a_re, a_im, bu_re, bu_im), axis=1)
    y = (jnp.einsum('blgp,ghp->blgh', x_re, c_re.astype(f32))
         - jnp.einsum('blgp,ghp->blgh', x_im, c_im.astype(f32)))
    y = y.reshape(bsz, L, SSM_WIDTH) + d_skip.astype(f32) * uf
    return y.astype(u.dtype), x_re[:, -1], x_im[:, -1]


def token_mixer(u, lp, attend, s0_re, s0_im):
    bsz, L, _ = u.shape
    q, k, v, xs, ga, gb = jnp.split(u @ lp['w_in'], IN_SPLITS, axis=-1)
    q = q.reshape(bsz, L, SB_HEADS, SB_HEAD_DIM)
    k = k.reshape(bsz, L, SB_HEADS, SB_HEAD_DIM)
    v = v.reshape(bsz, L, SB_HEADS, SB_HEAD_DIM)
    o_a = attend(q, k, v, lp['sb_bias']).reshape(bsz, L, SB_WIDTH)
    y, sr, si = s5_ssm(xs, lp['ssm_lambda_re'], lp['ssm_lambda_im'], lp['ssm_log_dt'],
                       lp['ssm_b_re'], lp['ssm_b_im'], lp['ssm_c_re'], lp['ssm_c_im'],
                       lp['ssm_d'], s0_re, s0_im)
    zg = jax.nn.gelu(y)
    o_b = zg * jax.nn.sigmoid(zg @ lp['glu_w'] + lp['glu_b'])
    merged = (jax.nn.sigmoid(ga) * (o_a @ lp['w_branch_a'])
              + jax.nn.sigmoid(gb) * (o_b @ lp['w_branch_b']))
    return merged @ lp['w_out'], k, v, sr, si


def decoder_layer(x, c, lp, attend, s0_re, s0_im):
    mod = jax.nn.silu(c) @ lp['ada_w'] + lp['ada_b']
    sh1, sc1, g1, sh2, sc2, g2, sh3, sc3, g3 = jnp.split(mod, ADA_CHUNKS, axis=-1)
    h = x + 0.5 * g1[:, None, :] * swiglu(rms_modulate(x, lp['norm_ffn1'], sh1, sc1),
                                          lp['ffn1_w_in'], lp['ffn1_w_out'])
    mix, k, v, sr, si = token_mixer(rms_modulate(h, lp['norm_mix'], sh2, sc2), lp, attend, s0_re, s0_im)
    h = h + g2[:, None, :] * mix
    h = h + 0.5 * g3[:, None, :] * swiglu(rms_modulate(h, lp['norm_ffn2'], sh3, sc3),
                                          lp['ffn2_w_in'], lp['ffn2_w_out'])
    return h, k, v, sr, si


def setup_inputs(seed: int = 0) -> dict:
    key = jax.random.key(seed)
    ks = iter(jax.random.split(key, 40))
    f32 = jnp.float32

    def nrm(shape, scale=1.0):
        return scale * jax.random.normal(next(ks), shape, f32)

    n_pages = PAST_LEN // PAGE_SIZE
    n_used = DEC_BATCH * n_pages
    n_pool = n_used + max(1, n_used // 4)
    cache_shape = (DEPTH, n_pool, PAGE_SIZE, SB_HEADS, SB_HEAD_DIM)
    state_shape = (DEPTH, DEC_BATCH, SSM_GROUPS, SSM_STATE)
    x_prompt = nrm((BATCH, SEQ, D_MODEL))
    x_sample = nrm((DEC_BATCH, DEC_SEQ, D_MODEL))
    c_prompt = nrm((BATCH, D_MODEL))
    c_sample = nrm((DEC_BATCH, D_MODEL))
    cache_k = nrm(cache_shape)
    cache_v = nrm(cache_shape)
    state_ssm_re = nrm(state_shape, 0.1)
    state_ssm_im = nrm(state_shape, 0.1)
    page_table = jax.random.permutation(next(ks), n_pool)[:n_used].reshape(DEC_BATCH, n_pages).astype(jnp.int32)
    gs = (DEPTH, SSM_GROUPS, SSM_STATE)
    lam_im_base = jnp.broadcast_to(jnp.pi * jnp.arange(SSM_STATE, dtype=f32), gs)
    return {
        'x_prompt': x_prompt, 'x_sample': x_sample,
        'c_prompt': c_prompt, 'c_sample': c_sample,
        'cache_k': cache_k, 'cache_v': cache_v,
        'state_ssm_re': state_ssm_re, 'state_ssm_im': state_ssm_im,
        'page_table': page_table,
        'ada_w': nrm((DEPTH, D_MODEL, ADA_CHUNKS * D_MODEL), D_MODEL ** -0.5),
        'ada_b': nrm((DEPTH, ADA_CHUNKS * D_MODEL), 0.01),
        'norm_ffn1': 1.0 + nrm((DEPTH, D_MODEL), 0.01),
        'ffn1_w_in': nrm((DEPTH, D_MODEL, 2 * D_FF), D_MODEL ** -0.5),
        'ffn1_w_out': nrm((DEPTH, D_FF, D_MODEL), D_FF ** -0.5),
        'norm_mix': 1.0 + nrm((DEPTH, D_MODEL), 0.01),
        'w_in': nrm((DEPTH, D_MODEL, IN_WIDTH), D_MODEL ** -0.5),
        'sb_bias': SB_BIAS_INIT + nrm((DEPTH, SB_HEADS), 0.1),
        'ssm_lambda_re': -0.5 + nrm(gs, 0.01),
        'ssm_lambda_im': lam_im_base + nrm(gs, 0.01),
        'ssm_log_dt': jax.random.uniform(next(ks), (DEPTH, SSM_GROUPS), f32, math.log(DT_MIN), math.log(DT_MAX)),
        'ssm_b_re': nrm((DEPTH, SSM_GROUPS, SSM_STATE, SSM_GROUP), (2 * SSM_GROUP) ** -0.5),
        'ssm_b_im': nrm((DEPTH, SSM_GROUPS, SSM_STATE, SSM_GROUP), (2 * SSM_GROUP) ** -0.5),
        'ssm_c_re': nrm((DEPTH, SSM_GROUPS, SSM_GROUP, SSM_STATE), SSM_STATE ** -0.5),
        'ssm_c_im': nrm((DEPTH, SSM_GROUPS, SSM_GROUP, SSM_STATE), SSM_STATE ** -0.5),
        'ssm_d': nrm((DEPTH, SSM_WIDTH)),
        'glu_w': nrm((DEPTH, SSM_WIDTH, SSM_WIDTH), SSM_WIDTH ** -0.5),
        'glu_b': nrm((DEPTH, SSM_WIDTH), 0.01),
        'w_branch_a': nrm((DEPTH, SB_WIDTH, D_MODEL), SB_WIDTH ** -0.5),
        'w_branch_b': nrm((DEPTH, SSM_WIDTH, D_MODEL), SSM_WIDTH ** -0.5),
        'w_out': nrm((DEPTH, D_MODEL, D_MODEL), D_MODEL ** -0.5),
        'norm_ffn2': 1.0 + nrm((DEPTH, D_MODEL), 0.01),
        'ffn2_w_in': nrm((DEPTH, D_MODEL, 2 * D_FF), D_MODEL ** -0.5),
        'ffn2_w_out': nrm((DEPTH, D_FF, D_MODEL), D_FF ** -0.5),
        'final_norm': 1.0 + nrm((D_MODEL,), 0.01),
    }


def reference(x_prompt, x_sample, c_prompt, c_sample, cache_k, cache_v, state_ssm_re, state_ssm_im,
              page_table, ada_w, ada_b, norm_ffn1, ffn1_w_in, ffn1_w_out, norm_mix, w_in, sb_bias,
              ssm_lambda_re, ssm_lambda_im, ssm_log_dt, ssm_b_re, ssm_b_im, ssm_c_re, ssm_c_im,
              ssm_d, glu_w, glu_b, w_branch_a, w_branch_b, w_out, norm_ffn2, ffn2_w_in,
              ffn2_w_out, final_norm):
    dec_b, n_pages = page_table.shape
    past_len = n_pages * PAGE_SIZE
    hp, hs = x_prompt, x_sample
    zero_state = jnp.zeros((x_prompt.shape[0], SSM_GROUPS, SSM_STATE), x_prompt.dtype)
    kp_l, vp_l, srp_l, sip_l, ks_l, vs_l, srs_l, sis_l = [], [], [], [], [], [], [], []
    for l in range(DEPTH):
        lp = dict(ada_w=ada_w[l], ada_b=ada_b[l], norm_ffn1=norm_ffn1[l], ffn1_w_in=ffn1_w_in[l],
                  ffn1_w_out=ffn1_w_out[l], norm_mix=norm_mix[l], w_in=w_in[l], sb_bias=sb_bias[l],
                  ssm_lambda_re=ssm_lambda_re[l], ssm_lambda_im=ssm_lambda_im[l],
                  ssm_log_dt=ssm_log_dt[l], ssm_b_re=ssm_b_re[l], ssm_b_im=ssm_b_im[l],
                  ssm_c_re=ssm_c_re[l], ssm_c_im=ssm_c_im[l], ssm_d=ssm_d[l], glu_w=glu_w[l],
                  glu_b=glu_b[l], w_branch_a=w_branch_a[l], w_branch_b=w_branch_b[l], w_out=w_out[l],
                  norm_ffn2=norm_ffn2[l], ffn2_w_in=ffn2_w_in[l], ffn2_w_out=ffn2_w_out[l])
        past_k = cache_k[l][page_table].reshape(dec_b, past_len, SB_HEADS, SB_HEAD_DIM)
        past_v = cache_v[l][page_table].reshape(dec_b, past_len, SB_HEADS, SB_HEAD_DIM)
        hp, kp, vp, srp, sip = decoder_layer(hp, c_prompt, lp, sb_prompt, zero_state, zero_state)
        hs, ks, vs, srs, sis = decoder_layer(hs, c_sample, lp,
                                             functools.partial(sb_sample, past_k=past_k, past_v=past_v),
                                             state_ssm_re[l], state_ssm_im[l])
        kp_l.append(kp); vp_l.append(vp); srp_l.append(srp); sip_l.append(sip)
        ks_l.append(ks); vs_l.append(vs); srs_l.append(srs); sis_l.append(sis)
    y_prompt = rmsnorm(hp, final_norm)
    y_sample = rmsnorm(hs, final_norm)
    return (y_prompt, y_sample,
            jnp.stack(kp_l), jnp.stack(vp_l), jnp.stack(srp_l), jnp.stack(sip_l),
            jnp.stack(ks_l), jnp.stack(vs_l), jnp.stack(srs_l), jnp.stack(sis_l))
```

```python
import functools

import jax
import jax.numpy as jnp
from jax import lax
from jax.experimental import pallas as pl
from jax.experimental.pallas import tpu as pltpu

F32 = jnp.float32
BF16 = jnp.bfloat16

RMS_EPS = 1e-6
N_HEADS = 8
HEAD_DIM = 64
LANES = 128
SSM_GROUP = 16
SSM_STATE = 64
SLAB = LANES
VMEM_LIMIT = 56 * 1024 * 1024


def _params(*sem):
    return pltpu.CompilerParams(dimension_semantics=sem, vmem_limit_bytes=VMEM_LIMIT)


def _dot(a, b):
    return jnp.dot(a, b, preferred_element_type=F32)


def _dot_nt(a, b):
    return lax.dot_general(a, b, (((1,), (1,)), ((), ())), preferred_element_type=F32)


def _rms(x):
    return x * lax.rsqrt(jnp.mean(x * x, axis=-1, keepdims=True) + RMS_EPS)


def _rms_mod(x, gain, shift, scale):
    return _rms(x) * gain * (1.0 + scale) + shift


def _mod_spec(per_row, tm, rows_per_seq, chunk, d):
    if per_row:
        return pl.BlockSpec((tm, d), lambda i, *_: (i, chunk))
    return pl.BlockSpec((None, 1, d), lambda i, *_: (i * tm // rows_per_seq, 0, chunk))


def _largest_tile(n, cap, mult):
    best = None
    for t in range(mult, min(n, cap) + 1, mult):
        if n % t == 0:
            best = t
    assert best is not None, (n, cap, mult)
    return best


def _ada_kernel(c_ref, w_ref, b_ref, o_ref):
    c = c_ref[...]
    a = (c * jax.nn.sigmoid(c)).astype(BF16)
    o_ref[...] = _dot(a, w_ref[...].astype(BF16)) + b_ref[...]


def _ada(c, ada_w, ada_b):
    rows, d = c.shape
    n = ada_w.shape[1]
    tn = _largest_tile(n, 1536, LANES)
    return pl.pallas_call(
        _ada_kernel,
        out_shape=jax.ShapeDtypeStruct((rows, n), F32),
        grid=(n // tn,),
        in_specs=[pl.BlockSpec((rows, d), lambda j: (0, 0)),
                  pl.BlockSpec((d, tn), lambda j: (0, j)),
                  pl.BlockSpec((1, tn), lambda j: (0, j))],
        out_specs=pl.BlockSpec((rows, tn), lambda j: (0, j)),
        compiler_params=_params("arbitrary"),
        name="ada",
    )(c, ada_w, ada_b.reshape(1, n))


def _ffn_kernel(x_ref, sh_ref, sc_ref, g_ref, gain_ref, wg_ref, wu_ref, wo_ref, fin_ref, o_ref,
                u_sc, acc_sc, *, final):
    j = pl.program_id(1)
    nj = pl.num_programs(1)

    @pl.when(j == 0)
    def _():
        u_sc[...] = _rms_mod(x_ref[...], gain_ref[...], sh_ref[...], sc_ref[...]).astype(BF16)

    u = u_sc[...]
    hg = _dot(u, wg_ref[...])
    hu = _dot(u, wu_ref[...])
    act = (hg * jax.nn.sigmoid(hg) * hu).astype(BF16)
    part = _dot(act, wo_ref[...])

    @pl.when(j == 0)
    def _():
        acc_sc[...] = part

    @pl.when(j > 0)
    def _():
        acc_sc[...] += part

    @pl.when(j == nj - 1)
    def _():
        h = x_ref[...] + 0.5 * g_ref[...] * acc_sc[...]
        if final:
            h = _rms(h) * fin_ref[...]
        o_ref[...] = h


def _ffn(x, mod, per_row, rows_per_seq, chunk0, gain, w_in, w_out, fin, *, final, tm):
    t, d = x.shape
    f = w_out.shape[0]
    tf = _largest_tile(f, 1408, LANES)
    nj = f // tf
    mspec = functools.partial(_mod_spec, per_row, tm, rows_per_seq, d=d)
    return pl.pallas_call(
        functools.partial(_ffn_kernel, final=final),
        out_shape=jax.ShapeDtypeStruct((t, d), F32),
        grid=(t // tm, nj),
        in_specs=[pl.BlockSpec((tm, d), lambda i, j: (i, 0)),
                  mspec(chunk0), mspec(chunk0 + 1), mspec(chunk0 + 2),
                  pl.BlockSpec((1, d), lambda i, j: (0, 0)),
                  pl.BlockSpec((d, tf), lambda i, j: (0, j)),
                  pl.BlockSpec((d, tf), lambda i, j: (0, nj + j)),
                  pl.BlockSpec((tf, d), lambda i, j: (j, 0)),
                  pl.BlockSpec((1, d), lambda i, j: (0, 0))],
        out_specs=pl.BlockSpec((tm, d), lambda i, j: (i, 0)),
        scratch_shapes=[pltpu.VMEM((tm, d), BF16), pltpu.VMEM((tm, d), F32)],
        compiler_params=_params("parallel", "arbitrary"),
        name="ffn_final" if final else "ffn",
    )(x, mod, mod, mod, gain, w_in, w_in, w_out, fin)


def _inproj_kernel(h_ref, sh_ref, sc_ref, gain_ref, w_ref,
                   q_ref, k_ref, v_ref, kb_ref, vb_ref, s_ref, ga_ref, gb_ref, *, sbw, ssw, d):
    u = _rms_mod(h_ref[...], gain_ref[...], sh_ref[...], sc_ref[...]).astype(BF16)
    o = 0
    q = _dot(u, w_ref[:, o:o + sbw]); o += sbw
    k = _dot(u, w_ref[:, o:o + sbw]); o += sbw
    v = _dot(u, w_ref[:, o:o + sbw]); o += sbw
    q_ref[...] = (q * (HEAD_DIM ** -0.5)).astype(BF16)
    k_ref[...] = k
    v_ref[...] = v
    kb_ref[...] = k.astype(BF16)
    vb_ref[...] = v.astype(BF16)
    s_ref[...] = _dot(u, w_ref[:, o:o + ssw]); o += ssw
    ga_ref[...] = _dot(u, w_ref[:, o:o + d]); o += d
    gb_ref[...] = _dot(u, w_ref[:, o:o + d])


def _inproj(h, mod, per_row, rows_per_seq, gain, w, *, tm):
    t, d = h.shape
    sbw = N_HEADS * HEAD_DIM
    ssw = d // 2
    n = w.shape[1]
    assert n == 3 * sbw + ssw + 2 * d
    mspec = functools.partial(_mod_spec, per_row, tm, rows_per_seq, d=d)
    row = lambda width: pl.BlockSpec((tm, width), lambda i: (i, 0))
    sds = lambda width, dt: jax.ShapeDtypeStruct((t, width), dt)
    return pl.pallas_call(
        functools.partial(_inproj_kernel, sbw=sbw, ssw=ssw, d=d),
        out_shape=(sds(sbw, BF16), sds(sbw, F32), sds(sbw, F32), sds(sbw, BF16), sds(sbw, BF16),
                   sds(ssw, F32), sds(d, F32), sds(d, F32)),
        grid=(t // tm,),
        in_specs=[row(d), mspec(3), mspec(4),
                  pl.BlockSpec((1, d), lambda i: (0, 0)),
                  pl.BlockSpec((d, n), lambda i: (0, 0))],
        out_specs=(row(sbw), row(sbw), row(sbw), row(sbw), row(sbw), row(ssw), row(d), row(d)),
        compiler_params=_params("parallel"),
        name="inproj",
    )(h, mod, mod, gain, w)


def _sb_tile(z, cs_mat, carry, mask):
    l = jnp.log(1.0 + jnp.exp(-jnp.abs(z)))
    log_beta = jnp.minimum(z, 0.0) - l
    log_keep = -jnp.maximum(z, 0.0) - l
    if mask is not None:
        log_keep = jnp.where(mask, log_keep, 0.0)
    hi = log_keep.astype(BF16)
    lo = (log_keep - hi.astype(F32)).astype(BF16)
    after_in = _dot(hi, cs_mat) + _dot(lo, cs_mat)
    w = jnp.exp(log_beta + after_in + carry)
    if mask is not None:
        w = jnp.where(mask, w, 0.0)
    total = after_in[:, 0:1] + log_keep[:, 0:1]
    return w, carry + total


def _cs_matrix(n):
    r = lax.broadcasted_iota(jnp.int32, (n, n), 0)
    c = lax.broadcasted_iota(jnp.int32, (n, n), 1)
    return jnp.where(r > c, 1.0, 0.0).astype(BF16)


def _attn_prompt_kernel(bias_ref, q_ref, k_ref, v_ref, o_ref, *, tq):
    p = pl.program_id(1)
    i = pl.program_id(2)
    q2 = q_ref[...]
    lane = lax.broadcasted_iota(jnp.int32, (tq, LANES), 1)
    cs_mat = _cs_matrix(tq)
    r = lax.broadcasted_iota(jnp.int32, (tq, tq), 0)
    c = lax.broadcasted_iota(jnp.int32, (tq, tq), 1)
    diag_mask = c < r
    accs = []
    for hh in range(2):
        in_head = (lane >= hh * HEAD_DIM) & (lane < (hh + 1) * HEAD_DIM)
        qm = jnp.where(in_head, q2, jnp.zeros_like(q2))
        bias = bias_ref[2 * p + hh]

        def tile(kb, carry, acc, mask):
            start = pl.multiple_of(kb * tq, tq)
            z = _dot_nt(qm, k_ref[pl.ds(start, tq), :]) + bias
            w, carry = _sb_tile(z, cs_mat, carry, mask)
            acc = acc + _dot(w.astype(BF16), v_ref[pl.ds(start, tq), :])
            return carry, acc

        carry0 = jnp.zeros((tq, 1), F32)
        acc0 = jnp.zeros((tq, LANES), F32)
        carry, acc = tile(i, carry0, acc0, diag_mask)

        def body(n, st):
            return tile(i - 1 - n, st[0], st[1], None)

        carry, acc = lax.fori_loop(0, i, body, (carry, acc))
        accs.append(acc)
    o_ref[...] = jnp.where(lane < HEAD_DIM, accs[0], accs[1]).astype(o_ref.dtype)


def _attn_prompt(q, k, v, bias, *, tq):
    b, s, w = q.shape
    npair = w // LANES
    return pl.pallas_call(
        functools.partial(_attn_prompt_kernel, tq=tq),
        out_shape=jax.ShapeDtypeStruct((b, s, w), BF16),
        grid_spec=pltpu.PrefetchScalarGridSpec(
            num_scalar_prefetch=0,
            grid=(b, npair, s // tq),
            in_specs=[pl.BlockSpec(memory_space=pltpu.SMEM),
                      pl.BlockSpec((None, tq, LANES), lambda bi, p, i: (bi, i, p)),
                      pl.BlockSpec((None, s, LANES), lambda bi, p, i: (bi, 0, p)),
                      pl.BlockSpec((None, s, LANES), lambda bi, p, i: (bi, 0, p))],
            out_specs=pl.BlockSpec((None, tq, LANES), lambda bi, p, i: (bi, i, p))),
        compiler_params=_params("parallel", "parallel", "arbitrary"),
        name="attn_prompt",
    )(bias, q, k, v)


def _attn_sample_kernel(pt_ref, bias_ref, q_ref, kn_ref, vn_ref, kc_ref, vc_ref, o_ref,
                        qbd_sc, carry_sc, acc_sc, *, t_new, page):
    j = pl.program_id(1)
    nj = pl.num_programs(1)
    rows = N_HEADS * t_new
    width = N_HEADS * HEAD_DIM
    lane = lax.broadcasted_iota(jnp.int32, (t_new, width), 1)
    cs_mat = _cs_matrix(page)
    bias_col = jnp.concatenate(
        [jnp.full((t_new, 1), bias_ref[h], F32) for h in range(N_HEADS)], axis=0)

    def tile(kblk, vblk, mask):
        z = _dot_nt(qbd_sc[...].astype(BF16), kblk) + bias_col
        w, carry = _sb_tile(z, cs_mat, carry_sc[...], mask)
        carry_sc[...] = carry
        acc_sc[...] += _dot(w.astype(BF16), vblk)

    @pl.when(j == 0)
    def _():
        q = q_ref[...]
        for h in range(N_HEADS):
            in_head = (lane >= h * HEAD_DIM) & (lane < (h + 1) * HEAD_DIM)
            qbd_sc[h * t_new:(h + 1) * t_new, :] = jnp.where(in_head, q, jnp.zeros_like(q))
        carry_sc[...] = jnp.zeros_like(carry_sc)
        acc_sc[...] = jnp.zeros_like(acc_sc)
        rq = lax.broadcasted_iota(jnp.int32, (rows, page), 0) % t_new
        ck = lax.broadcasted_iota(jnp.int32, (rows, page), 1)
        tile(kn_ref[...], vn_ref[...], ck < rq)

    tile(kc_ref[...], vc_ref[...], None)

    @pl.when(j == nj - 1)
    def _():
        acc = acc_sc[...]
        out = jnp.zeros((t_new, width), F32)
        for h in range(N_HEADS):
            in_head = (lane >= h * HEAD_DIM) & (lane < (h + 1) * HEAD_DIM)
            out = out + jnp.where(in_head, acc[h * t_new:(h + 1) * t_new, :], 0.0)
        o_ref[...] = out.astype(o_ref.dtype)


def _attn_sample(q, k_new, v_new, cache_k, cache_v, page_table, bias):
    nb, t_new, width = q.shape
    n_pages = page_table.shape[1]
    page = cache_k.shape[1]
    rows = N_HEADS * t_new
    pad = ((0, 0), (0, page - t_new), (0, 0))
    k_new = jnp.pad(k_new, pad)
    v_new = jnp.pad(v_new, pad)
    seq = lambda rws: pl.BlockSpec((None, rws, width), lambda b, j, pt: (b, 0, 0))
    cache = pl.BlockSpec((None, page, width), lambda b, j, pt: (pt[b, n_pages - 1 - j], 0, 0))
    return pl.pallas_call(
        functools.partial(_attn_sample_kernel, t_new=t_new, page=page),
        out_shape=jax.ShapeDtypeStruct((nb, t_new, width), F32),
        grid_spec=pltpu.PrefetchScalarGridSpec(
            num_scalar_prefetch=1,
            grid=(nb, n_pages),
            in_specs=[pl.BlockSpec(memory_space=pltpu.SMEM),
                      seq(t_new), seq(page), seq(page), cache, cache],
            out_specs=seq(t_new),
            scratch_shapes=[pltpu.VMEM((rows, width), F32),
                            pltpu.VMEM((rows, 1), F32),
                            pltpu.VMEM((rows, width), F32)]),
        compiler_params=_params("parallel", "arbitrary"),
        name="attn_sample",
    )(page_table, bias, q, k_new, v_new, cache_k, cache_v)


def _ssm_kernel(s_ref, x0r_ref, x0i_ref, ar_ref, ai_ref, bz_ref, cz_ref, d_ref, gw_ref, gb_ref,
                o_ref, xr_ref, xi_ref, bur_sc, bui_sc, *, bs, tt, time_outer):
    step = pl.program_id(0)
    rows = bs * tt
    pitch = rows + 8
    n_slab = ar_ref.shape[0]
    width = s_ref.shape[-1]

    def init():
        xr_ref[...] = x0r_ref[...]
        xi_ref[...] = x0i_ref[...]

    if time_outer:
        pl.when(step == 0)(init)
    else:
        init()

    s = s_ref[...].reshape(rows, width)
    sb = s.astype(BF16)
    per_block = LANES // (2 * SSM_GROUP)
    for c in range(n_slab):
        blk = c // per_block
        bu = _dot(sb[:, blk * LANES:(blk + 1) * LANES], bz_ref[c])
        bur_sc[c * pitch:c * pitch + rows, :] = bu[:, :SLAB]
        bui_sc[c * pitch:c * pitch + rows, :] = bu[:, SLAB:]

    ar = ar_ref[...]
    ai = ai_ref[...]

    def one(row, xr, xi):
        idx = pl.ds(row, n_slab, stride=pitch)
        nr = ar * xr - ai * xi + bur_sc[idx, :]
        ni = ar * xi + ai * xr + bui_sc[idx, :]
        bur_sc[idx, :] = nr
        bui_sc[idx, :] = ni
        return nr, ni

    if time_outer:
        def body(t, st):
            out = []
            for b in range(bs):
                out.append(one(b * tt + t, st[b][0], st[b][1]))
            return tuple(out)

        st = lax.fori_loop(0, tt, body, tuple((xr_ref[b], xi_ref[b]) for b in range(bs)))
        for b in range(bs):
            xr_ref[b] = st[b][0]
            xi_ref[b] = st[b][1]
    else:
        def body(b, _):
            xr, xi = xr_ref[b], xi_ref[b]
            for t in range(tt):
                xr, xi = one(b * tt + t, xr, xi)
            xr_ref[b] = xr
            xi_ref[b] = xi
            return 0

        lax.fori_loop(0, bs, body, 0)

    ys = []
    for blk in range(width // LANES):
        y = jnp.zeros((rows, LANES), F32)
        for c in range(blk * per_block, (blk + 1) * per_block):
            x2 = jnp.concatenate([bur_sc[c * pitch:c * pitch + rows, :],
                                  bui_sc[c * pitch:c * pitch + rows, :]], axis=1).astype(BF16)
            y = y + _dot(x2, cz_ref[c])
        ys.append(y)
    y = jnp.concatenate(ys, axis=1) + d_ref[...] * s
    zg = jax.nn.gelu(y)
    gate = jax.nn.sigmoid(_dot(zg.astype(BF16), gw_ref[...]) + gb_ref[...])
    o_ref[...] = (zg * gate).astype(o_ref.dtype).reshape(o_ref.shape)


def _ssm_tables(lam_re, lam_im, log_dt, b_re, b_im, c_re, c_im):
    g, p = lam_re.shape
    hch = b_re.shape[-1]
    dt = jnp.exp(log_dt)[:, None]
    mag = jnp.exp(lam_re * dt)
    ab_re, ab_im = mag * jnp.cos(lam_im * dt), mag * jnp.sin(lam_im * dt)
    den = lam_re * lam_re + lam_im * lam_im
    nr, ni = ab_re - 1.0, ab_im
    zr, zi = (nr * lam_re + ni * lam_im) / den, (ni * lam_re - nr * lam_im) / den
    bb_re = zr[..., None] * b_re - zi[..., None] * b_im
    bb_im = zr[..., None] * b_im + zi[..., None] * b_re
    n_slab = g * p // SLAB
    gps = SLAB // p
    per_block = LANES // (gps * hch)
    eye_g = jnp.eye(gps, dtype=F32)
    eye_b = jnp.eye(per_block, dtype=F32)

    def in_mat(bb):
        x = bb.reshape(n_slab, gps, p, hch)
        m = jnp.einsum('cepz,ef->cezfp', x, eye_g).reshape(n_slab, gps * hch, SLAB)
        sel = eye_b[jnp.arange(n_slab) % per_block]
        return jnp.einsum('crn,cq->cqrn', m, sel).reshape(n_slab, LANES, SLAB)

    def out_mat(cc):
        x = cc.reshape(n_slab, gps, hch, p)
        m = jnp.einsum('cezp,ef->cepfz', x, eye_g).reshape(n_slab, SLAB, gps * hch)
        sel = eye_b[jnp.arange(n_slab) % per_block]
        return jnp.einsum('cnr,cq->cnqr', m, sel).reshape(n_slab, SLAB, LANES)

    bz = jnp.concatenate([in_mat(bb_re), in_mat(bb_im)], axis=2).astype(BF16)
    cz = jnp.concatenate([out_mat(c_re), -out_mat(c_im)], axis=1).astype(BF16)
    return ab_re.reshape(n_slab, SLAB), ab_im.reshape(n_slab, SLAB), bz, cz


def _ssm(s, x0r, x0i, tables, d_skip, glu_w, glu_b, *, bs, tt, time_outer, out_dtype):
    nseq, length, width = s.shape
    ar, ai, bz, cz = tables
    n_slab = ar.shape[0]
    rows = bs * tt
    pitch = rows + 8
    if time_outer:
        assert bs == nseq
        grid = (length // tt,)
        smap = lambda i: (0, i, 0)
        xmap = lambda i: (0, 0, 0)
    else:
        assert tt == length
        grid = (nseq // bs,)
        smap = lambda i: (i, 0, 0)
        xmap = lambda i: (i, 0, 0)
    const2 = lambda shape: pl.BlockSpec(shape, lambda i: (0, 0))
    const3 = lambda shape: pl.BlockSpec(shape, lambda i: (0, 0, 0))
    xspec = pl.BlockSpec((bs, n_slab, SLAB), xmap)
    return pl.pallas_call(
        functools.partial(_ssm_kernel, bs=bs, tt=tt, time_outer=time_outer),
        out_shape=(jax.ShapeDtypeStruct((nseq, length, width), out_dtype),
                   jax.ShapeDtypeStruct((nseq, n_slab, SLAB), F32),
                   jax.ShapeDtypeStruct((nseq, n_slab, SLAB), F32)),
        grid=grid,
        in_specs=[pl.BlockSpec((bs, tt, width), smap), xspec, xspec,
                  const2((n_slab, SLAB)), const2((n_slab, SLAB)),
                  const3(bz.shape), const3(cz.shape),
                  const2((1, width)), const2(glu_w.shape), const2((1, width))],
        out_specs=(pl.BlockSpec((bs, tt, width), smap), xspec, xspec),
        scratch_shapes=[pltpu.VMEM((n_slab * pitch, SLAB), F32),
                        pltpu.VMEM((n_slab * pitch, SLAB), F32)],
        compiler_params=_params("arbitrary"),
        name="ssm",
    )(s, x0r, x0i, ar, ai, bz, cz, d_skip.reshape(1, width), glu_w, glu_b.reshape(1, width))


def _mix_kernel(oa_ref, ob_ref, ga_ref, gb_ref, h_ref, g2_ref, wa_ref, wb_ref, wo_ref, o_ref):
    merged = (jax.nn.sigmoid(ga_ref[...]) * _dot(oa_ref[...].astype(BF16), wa_ref[...])
              + jax.nn.sigmoid(gb_ref[...]) * _dot(ob_ref[...].astype(BF16), wb_ref[...]))
    mix = _dot(merged.astype(BF16), wo_ref[...])
    o_ref[...] = h_ref[...] + g2_ref[...] * mix


def _mix(oa, ob, ga, gb, h, mod, per_row, rows_per_seq, wa, wb, wo, *, tm):
    t, d = h.shape
    w = oa.shape[1]
    row = lambda width: pl.BlockSpec((tm, width), lambda i: (i, 0))
    full = lambda a: pl.BlockSpec(a.shape, lambda i: (0, 0))
    return pl.pallas_call(
        _mix_kernel,
        out_shape=jax.ShapeDtypeStruct((t, d), F32),
        grid=(t // tm,),
        in_specs=[row(w), row(w), row(d), row(d), row(d),
                  _mod_spec(per_row, tm, rows_per_seq, 5, d),
                  full(wa), full(wb), full(wo)],
        out_specs=row(d),
        compiler_params=_params("parallel"),
        name="mix",
    )(oa, ob, ga, gb, h, mod, wa, wb, wo)


def kernel(x_prompt, x_sample, c_prompt, c_sample, cache_k, cache_v, state_ssm_re, state_ssm_im, page_table, ada_w, ada_b, norm_ffn1, ffn1_w_in, ffn1_w_out, norm_mix, w_in, sb_bias, ssm_lambda_re, ssm_lambda_im, ssm_log_dt, ssm_b_re, ssm_b_im, ssm_c_re, ssm_c_im, ssm_d, glu_w, glu_b, w_branch_a, w_branch_b, w_out, norm_ffn2, ffn2_w_in, ffn2_w_out, final_norm):
    assert ada_w.shape[0] == 1, "single-layer step"
    sq = lambda a: a.reshape(a.shape[1:])
    (ada_w, ada_b, norm_ffn1, ffn1_w_in, ffn1_w_out, norm_mix, w_in, sb_bias, ssm_lambda_re, ssm_lambda_im,
     ssm_log_dt, ssm_b_re, ssm_b_im, ssm_c_re, ssm_c_im, ssm_d, glu_w, glu_b, w_branch_a, w_branch_b, w_out,
     norm_ffn2, ffn2_w_in, ffn2_w_out, cache_k, cache_v, state_ssm_re, state_ssm_im) = map(sq, (
         ada_w, ada_b, norm_ffn1, ffn1_w_in, ffn1_w_out, norm_mix, w_in, sb_bias, ssm_lambda_re, ssm_lambda_im,
         ssm_log_dt, ssm_b_re, ssm_b_im, ssm_c_re, ssm_c_im, ssm_d, glu_w, glu_b, w_branch_a, w_branch_b, w_out,
         norm_ffn2, ffn2_w_in, ffn2_w_out, cache_k, cache_v, state_ssm_re, state_ssm_im))
    bp, sp, d = x_prompt.shape
    bsm, ssm_len, _ = x_sample.shape
    sbw = N_HEADS * HEAD_DIM
    n_pool, page = cache_k.shape[0], cache_k.shape[1]
    n_slab = ssm_lambda_re.shape[0] * ssm_lambda_re.shape[1] // SLAB

    n_c = bp + bsm
    pad_c = (-n_c) % 8
    c_all = jnp.concatenate([c_prompt, c_sample, jnp.zeros((pad_c, d), F32)], axis=0)
    mod = _ada(c_all, ada_w, ada_b)
    mod_p = mod[:bp].reshape(bp, 1, -1)
    mod_s = jnp.repeat(mod[bp:n_c], ssm_len, axis=0)

    bf = lambda a: a.astype(BF16)
    w1i, w1o, w2i, w2o = bf(ffn1_w_in), bf(ffn1_w_out), bf(ffn2_w_in), bf(ffn2_w_out)
    wi, wa, wb, wo, gw = bf(w_in), bf(w_branch_a), bf(w_branch_b), bf(w_out), bf(glu_w)
    row = lambda a: a.reshape(1, -1)
    tables = _ssm_tables(ssm_lambda_re, ssm_lambda_im, ssm_log_dt, ssm_b_re, ssm_b_im, ssm_c_re, ssm_c_im)
    cache_k2 = bf(cache_k).reshape(n_pool, page, sbw)
    cache_v2 = bf(cache_v).reshape(n_pool, page, sbw)

    outs = {}
    for name, x, m, per_row, nseq, length in (("p", x_prompt, mod_p, False, bp, sp),
                                              ("s", x_sample, mod_s, True, bsm, ssm_len)):
        t = nseq * length
        tm = _largest_tile(t, 512, 8)
        tm_in = _largest_tile(t, 256, 8)
        x2 = x.reshape(t, d)
        h1 = _ffn(x2, m, per_row, length, 0, row(norm_ffn1), w1i, w1o, row(final_norm), final=False, tm=tm)
        q, k, v, kb, vb, s, ga, gb = _inproj(h1, m, per_row, length, row(norm_mix), wi, tm=tm_in)
        s3 = s.reshape(nseq, length, -1)
        seq3 = lambda a: a.reshape(nseq, length, sbw)
        if name == "p":
            oa = _attn_prompt(seq3(q), seq3(kb), seq3(vb), sb_bias, tq=min(256, length))
            zero = jnp.zeros((nseq, n_slab, SLAB), F32)
            ob, xr, xi = _ssm(s3, zero, zero, tables, ssm_d, gw, glu_b,
                              bs=nseq, tt=min(64, length), time_outer=True, out_dtype=BF16)
        else:
            oa = _attn_sample(seq3(q).astype(F32), seq3(kb), seq3(vb), cache_k2, cache_v2, page_table, sb_bias)
            ob, xr, xi = _ssm(s3, state_ssm_re.reshape(nseq, n_slab, SLAB),
                              state_ssm_im.reshape(nseq, n_slab, SLAB), tables, ssm_d, gw, glu_b,
                              bs=min(16, nseq), tt=length, time_outer=False, out_dtype=F32)
        h2 = _mix(oa.reshape(t, sbw), ob.reshape(t, -1), ga, gb, h1, m, per_row, length, wa, wb, wo, tm=tm)
        y = _ffn(h2, m, per_row, length, 6, row(norm_ffn2), w2i, w2o, row(final_norm), final=True, tm=tm)
        st_shape = (1, nseq) + ssm_lambda_re.shape
        outs[name] = (y.reshape(nseq, length, d),
                      k.reshape(1, nseq, length, N_HEADS, HEAD_DIM),
                      v.reshape(1, nseq, length, N_HEADS, HEAD_DIM),
                      xr.reshape(st_shape), xi.reshape(st_shape))
    yp, kp, vp, srp, sip = outs["p"]
    ys, ks, vs, srs, sis = outs["s"]
    return (yp, ys, kp, vp, srp, sip, ks, vs, srs, sis)
```

```python
import functools

import jax
import jax.numpy as jnp
from jax import lax
from jax.experimental import pallas as pl
from jax.experimental.pallas import tpu as pltpu

F32 = jnp.float32
BF16 = jnp.bfloat16

RMS_EPS = 1e-6
N_HEADS = 8
HEAD_DIM = 64
LANES = 128
SSM_GROUP = 16
SSM_STATE = 64
SLAB = LANES
VMEM_LIMIT = 56 * 1024 * 1024


def _params(*sem):
    return pltpu.CompilerParams(dimension_semantics=sem, vmem_limit_bytes=VMEM_LIMIT)


def _dot(a, b):
    return jnp.dot(a, b, preferred_element_type=F32)


def _dot_nt(a, b):
    return lax.dot_general(a, b, (((1,), (1,)), ((), ())), preferred_element_type=F32)


def _rms(x):
    return x * lax.rsqrt(jnp.mean(x * x, axis=-1, keepdims=True) + RMS_EPS)


def _rms_mod(x, gain, shift, scale):
    return _rms(x) * gain * (1.0 + scale) + shift


def _mod_spec(per_row, tm, rows_per_seq, chunk, d):
    if per_row:
        return pl.BlockSpec((tm, d), lambda i, *_: (i, chunk))
    return pl.BlockSpec((None, 1, d), lambda i, *_: (i * tm // rows_per_seq, 0, chunk))


def _largest_tile(n, cap, mult):
    best = None
    for t in range(mult, min(n, cap) + 1, mult):
        if n % t == 0:
            best = t
    assert best is not None, (n, cap, mult)
    return best


def _ada_kernel(c_ref, w_ref, b_ref, o_ref):
    c = c_ref[...]
    a = (c * jax.nn.sigmoid(c)).astype(BF16)
    o_ref[...] = _dot(a, w_ref[...].astype(BF16)) + b_ref[...]


def _ada(c, ada_w, ada_b):
    rows, d = c.shape
    n = ada_w.shape[1]
    tn = _largest_tile(n, 1536, LANES)
    return pl.pallas_call(
        _ada_kernel,
        out_shape=jax.ShapeDtypeStruct((rows, n), F32),
        grid=(n // tn,),
        in_specs=[pl.BlockSpec((rows, d), lambda j: (0, 0)),
                  pl.BlockSpec((d, tn), lambda j: (0, j)),
                  pl.BlockSpec((1, tn), lambda j: (0, j))],
        out_specs=pl.BlockSpec((rows, tn), lambda j: (0, j)),
        compiler_params=_params("arbitrary"),
        name="ada",
    )(c, ada_w, ada_b.reshape(1, n))


def _ffn_kernel(x_ref, sh_ref, sc_ref, g_ref, gain_ref, wg_ref, wu_ref, wo_ref, fin_ref, o_ref,
                u_sc, acc_sc, *, final):
    j = pl.program_id(1)
    nj = pl.num_programs(1)

    @pl.when(j == 0)
    def _():
        u_sc[...] = _rms_mod(x_ref[...], gain_ref[...], sh_ref[...], sc_ref[...]).astype(BF16)

    u = u_sc[...]
    hg = _dot(u, wg_ref[...])
    hu = _dot(u, wu_ref[...])
    act = (hg * jax.nn.sigmoid(hg) * hu).astype(BF16)
    part = _dot(act, wo_ref[...])

    @pl.when(j == 0)
    def _():
        acc_sc[...] = part

    @pl.when(j > 0)
    def _():
        acc_sc[...] += part

    @pl.when(j == nj - 1)
    def _():
        h = x_ref[...] + 0.5 * g_ref[...] * acc_sc[...]
        if final:
            h = _rms(h) * fin_ref[...]
        o_ref[...] = h


def _ffn(x, mod, per_row, rows_per_seq, chunk0, gain, w_in, w_out, fin, *, final, tm):
    t, d = x.shape
    f = w_out.shape[0]
    tf = _largest_tile(f, 1408, LANES)
    nj = f // tf
    mspec = functools.partial(_mod_spec, per_row, tm, rows_per_seq, d=d)
    return pl.pallas_call(
        functools.partial(_ffn_kernel, final=final),
        out_shape=jax.ShapeDtypeStruct((t, d), F32),
        grid=(t // tm, nj),
        in_specs=[pl.BlockSpec((tm, d), lambda i, j: (i, 0)),
                  mspec(chunk0), mspec(chunk0 + 1), mspec(chunk0 + 2),
                  pl.BlockSpec((1, d), lambda i, j: (0, 0)),
                  pl.BlockSpec((d, tf), lambda i, j: (0, j)),
                  pl.BlockSpec((d, tf), lambda i, j: (0, nj + j)),
                  pl.BlockSpec((tf, d), lambda i, j: (j, 0)),
                  pl.BlockSpec((1, d), lambda i, j: (0, 0))],
        out_specs=pl.BlockSpec((tm, d), lambda i, j: (i, 0)),
        scratch_shapes=[pltpu.VMEM((tm, d), BF16), pltpu.VMEM((tm, d), F32)],
        compiler_params=_params("parallel", "arbitrary"),
        name="ffn_final" if final else "ffn",
    )(x, mod, mod, mod, gain, w_in, w_in, w_out, fin)


def _inproj_kernel(h_ref, sh_ref, sc_ref, gain_ref, w_ref,
                   q_ref, k_ref, v_ref, kb_ref, vb_ref, s_ref, ga_ref, gb_ref, *, sbw, ssw, d):
    u = _rms_mod(h_ref[...], gain_ref[...], sh_ref[...], sc_ref[...]).astype(BF16)
    o = 0
    q = _dot(u, w_ref[:, o:o + sbw]); o += sbw
    k = _dot(u, w_ref[:, o:o + sbw]); o += sbw
    v = _dot(u, w_ref[:, o:o + sbw]); o += sbw
    q_ref[...] = (q * (HEAD_DIM ** -0.5)).astype(BF16)
    k_ref[...] = k
    v_ref[...] = v
    kb_ref[...] = k.astype(BF16)
    vb_ref[...] = v.astype(BF16)
    s_ref[...] = _dot(u, w_ref[:, o:o + ssw]); o += ssw
    ga_ref[...] = _dot(u, w_ref[:, o:o + d]); o += d
    gb_ref[...] = _dot(u, w_ref[:, o:o + d])


def _inproj(h, mod, per_row, rows_per_seq, gain, w, *, tm):
    t, d = h.shape
    sbw = N_HEADS * HEAD_DIM
    ssw = d // 2
    n = w.shape[1]
    assert n == 3 * sbw + ssw + 2 * d
    mspec = functools.partial(_mod_spec, per_row, tm, rows_per_seq, d=d)
    row = lambda width: pl.BlockSpec((tm, width), lambda i: (i, 0))
    sds = lambda width, dt: jax.ShapeDtypeStruct((t, width), dt)
    return pl.pallas_call(
        functools.partial(_inproj_kernel, sbw=sbw, ssw=ssw, d=d),
        out_shape=(sds(sbw, BF16), sds(sbw, F32), sds(sbw, F32), sds(sbw, BF16), sds(sbw, BF16),
                   sds(ssw, F32), sds(d, F32), sds(d, F32)),
        grid=(t // tm,),
        in_specs=[row(d), mspec(3), mspec(4),
                  pl.BlockSpec((1, d), lambda i: (0, 0)),
                  pl.BlockSpec((d, n), lambda i: (0, 0))],
        out_specs=(row(sbw), row(sbw), row(sbw), row(sbw), row(sbw), row(ssw), row(d), row(d)),
        compiler_params=_params("parallel"),
        name="inproj",
    )(h, mod, mod, gain, w)


def _sb_drop(z, cs2, mask):
    sp = jnp.maximum(z, 0.0) + jnp.log(1.0 + jnp.exp(-jnp.abs(z)))
    log_beta = z - sp
    if mask is not None:
        sp = jnp.where(mask, sp, 0.0)
    hi = sp.astype(BF16)
    lo = (sp - hi.astype(F32)).astype(BF16)
    drop_in = _dot(jnp.concatenate([hi, lo], axis=1), cs2)
    return log_beta, drop_in, drop_in[:, 0:1] + sp[:, 0:1]


def _sb_weights(log_beta, drop_in, drop_later, mask):
    w = jnp.exp(log_beta - drop_in - drop_later)
    if mask is not None:
        w = jnp.where(mask, w, 0.0)
    return w


def _cs_matrix(n):
    r = lax.broadcasted_iota(jnp.int32, (2 * n, n), 0) % n
    c = lax.broadcasted_iota(jnp.int32, (2 * n, n), 1)
    return jnp.where(r > c, 1.0, 0.0).astype(BF16)


def _attn_prompt_kernel(bias_ref, q_ref, k_ref, v_ref, o_ref, *, tq, pairs):
    g = pl.program_id(1)
    i = pl.program_id(2)
    lane = lax.broadcasted_iota(jnp.int32, (tq, LANES), 1)
    cs2 = _cs_matrix(tq)
    r = lax.broadcasted_iota(jnp.int32, (tq, tq), 0)
    c = lax.broadcasted_iota(jnp.int32, (tq, tq), 1)
    diag_mask = c < r
    heads = []
    for pp in range(pairs):
        q2 = q_ref[:, pp * LANES:(pp + 1) * LANES]
        for hh in range(2):
            in_head = (lane >= hh * HEAD_DIM) & (lane < (hh + 1) * HEAD_DIM)
            heads.append((slice(pp * LANES, (pp + 1) * LANES), jnp.where(in_head, q2, jnp.zeros_like(q2)),
                          bias_ref[(g * pairs + pp) * 2 + hh]))

    def tiles(kb, state, mask):
        rows = pl.ds(pl.multiple_of(kb * tq, tq), tq)
        n = len(heads)
        zs, drops, out = {}, {}, []
        for step in range(n + 2):
            if step < n:
                cols, qm, bias = heads[step]
                zs[step] = _dot_nt(qm, k_ref[rows, cols]) + bias
            if 1 <= step <= n:
                drops[step - 1] = _sb_drop(zs.pop(step - 1), cs2, mask)
            if step >= 2:
                h = step - 2
                log_beta, drop_in, drop_tot = drops.pop(h)
                drop_later, acc = state[h]
                w = _sb_weights(log_beta, drop_in, drop_later, mask)
                out.append((drop_later + drop_tot, acc + _dot(w.astype(BF16), v_ref[rows, heads[h][0]])))
        return tuple(out)

    zero = (jnp.zeros((tq, 1), F32), jnp.zeros((tq, LANES), F32))
    state = tiles(i, (zero,) * len(heads), diag_mask)
    state = lax.fori_loop(0, i, lambda n, st: tiles(i - 1 - n, st, None), state)
    for pp in range(pairs):
        o_ref[:, pp * LANES:(pp + 1) * LANES] = jnp.where(
            lane < HEAD_DIM, state[2 * pp][1], state[2 * pp + 1][1]).astype(o_ref.dtype)


def _attn_prompt(q, k, v, bias, *, tq, pairs):
    b, s, w = q.shape
    wb = pairs * LANES
    return pl.pallas_call(
        functools.partial(_attn_prompt_kernel, tq=tq, pairs=pairs),
        out_shape=jax.ShapeDtypeStruct((b, s, w), BF16),
        grid=(b, w // wb, s // tq),
        in_specs=[pl.BlockSpec(memory_space=pltpu.SMEM),
                  pl.BlockSpec((None, tq, wb), lambda bi, g, i: (bi, i, g)),
                  pl.BlockSpec((None, s, wb), lambda bi, g, i: (bi, 0, g)),
                  pl.BlockSpec((None, s, wb), lambda bi, g, i: (bi, 0, g))],
        out_specs=pl.BlockSpec((None, tq, wb), lambda bi, g, i: (bi, i, g)),
        compiler_params=_params("parallel", "parallel", "arbitrary"),
        name="attn_prompt",
    )(bias, q, k, v)


def _attn_sample_kernel(pt_ref, bias_ref, q_ref, kn_ref, vn_ref, kc_ref, vc_ref, o_ref,
                        qbd_sc, carry_sc, acc_sc, *, t_new, page):
    j = pl.program_id(1)
    nj = pl.num_programs(1)
    rows = N_HEADS * t_new
    width = N_HEADS * HEAD_DIM
    lane = lax.broadcasted_iota(jnp.int32, (t_new, width), 1)
    cs2 = _cs_matrix(page)
    bias_col = jnp.concatenate(
        [jnp.full((t_new, 1), bias_ref[h], F32) for h in range(N_HEADS)], axis=0)

    def tile(kblk, vblk, mask):
        z = _dot_nt(qbd_sc[...].astype(BF16), kblk) + bias_col
        log_beta, drop_in, drop_tot = _sb_drop(z, cs2, mask)
        w = _sb_weights(log_beta, drop_in, carry_sc[...], mask)
        carry_sc[...] += drop_tot
        acc_sc[...] += _dot(w.astype(BF16), vblk)

    @pl.when(j == 0)
    def _():
        q = q_ref[...]
        for h in range(N_HEADS):
            in_head = (lane >= h * HEAD_DIM) & (lane < (h + 1) * HEAD_DIM)
            qbd_sc[h * t_new:(h + 1) * t_new, :] = jnp.where(in_head, q, jnp.zeros_like(q))
        carry_sc[...] = jnp.zeros_like(carry_sc)
        acc_sc[...] = jnp.zeros_like(acc_sc)
        rq = lax.broadcasted_iota(jnp.int32, (rows, page), 0) % t_new
        ck = lax.broadcasted_iota(jnp.int32, (rows, page), 1)
        tile(kn_ref[...], vn_ref[...], ck < rq)

    tile(kc_ref[...], vc_ref[...], None)

    @pl.when(j == nj - 1)
    def _():
        acc = acc_sc[...]
        out = jnp.zeros((t_new, width), F32)
        for h in range(N_HEADS):
            in_head = (lane >= h * HEAD_DIM) & (lane < (h + 1) * HEAD_DIM)
            out = out + jnp.where(in_head, acc[h * t_new:(h + 1) * t_new, :], 0.0)
        o_ref[...] = out.astype(o_ref.dtype)


def _attn_sample(q, k_new, v_new, cache_k, cache_v, page_table, bias):
    nb, t_new, width = q.shape
    n_pages = page_table.shape[1]
    page = cache_k.shape[1]
    rows = N_HEADS * t_new
    pad = ((0, 0), (0, page - t_new), (0, 0))
    k_new = jnp.pad(k_new, pad)
    v_new = jnp.pad(v_new, pad)
    seq = lambda rws: pl.BlockSpec((None, rws, width), lambda b, j, pt: (b, 0, 0))
    cache = pl.BlockSpec((None, page, width), lambda b, j, pt: (pt[b, n_pages - 1 - j], 0, 0))
    return pl.pallas_call(
        functools.partial(_attn_sample_kernel, t_new=t_new, page=page),
        out_shape=jax.ShapeDtypeStruct((nb, t_new, width), F32),
        grid_spec=pltpu.PrefetchScalarGridSpec(
            num_scalar_prefetch=1,
            grid=(nb, n_pages),
            in_specs=[pl.BlockSpec(memory_space=pltpu.SMEM),
                      seq(t_new), seq(page), seq(page), cache, cache],
            out_specs=seq(t_new),
            scratch_shapes=[pltpu.VMEM((rows, width), F32),
                            pltpu.VMEM((rows, 1), F32),
                            pltpu.VMEM((rows, width), F32)]),
        compiler_params=_params("parallel", "arbitrary"),
        name="attn_sample",
    )(page_table, bias, q, k_new, v_new, cache_k, cache_v)


def _ssm_kernel(s_ref, x0r_ref, x0i_ref, ar_ref, ai_ref, bz_ref, cz_ref, d_ref, gw_ref, gb_ref,
                o_ref, xr_ref, xi_ref, bur_sc, bui_sc, *, bs, tt, time_outer):
    step = pl.program_id(0)
    rows = bs * tt
    pitch = rows + 8
    n_slab = ar_ref.shape[0]
    width = s_ref.shape[-1]

    def init():
        xr_ref[...] = x0r_ref[...]
        xi_ref[...] = x0i_ref[...]

    if time_outer:
        pl.when(step == 0)(init)
    else:
        init()

    s = s_ref[...].reshape(rows, width)
    sb = s.astype(BF16)
    per_block = LANES // (2 * SSM_GROUP)
    for c in range(n_slab):
        blk = c // per_block
        bu = _dot(sb[:, blk * LANES:(blk + 1) * LANES], bz_ref[c])
        bur_sc[c * pitch:c * pitch + rows, :] = bu[:, :SLAB]
        bui_sc[c * pitch:c * pitch + rows, :] = bu[:, SLAB:]

    ar = ar_ref[...]
    ai = ai_ref[...]

    def one(row, xr, xi):
        idx = pl.ds(row, n_slab, stride=pitch)
        nr = ar * xr - ai * xi + bur_sc[idx, :]
        ni = ar * xi + ai * xr + bui_sc[idx, :]
        bur_sc[idx, :] = nr
        bui_sc[idx, :] = ni
        return nr, ni

    if time_outer:
        def body(t, st):
            out = []
            for b in range(bs):
                out.append(one(b * tt + t, st[b][0], st[b][1]))
            return tuple(out)

        st = lax.fori_loop(0, tt, body, tuple((xr_ref[b], xi_ref[b]) for b in range(bs)))
        for b in range(bs):
            xr_ref[b] = st[b][0]
            xi_ref[b] = st[b][1]
    else:
        def body(b, _):
            xr, xi = xr_ref[b], xi_ref[b]
            for t in range(tt):
                xr, xi = one(b * tt + t, xr, xi)
            xr_ref[b] = xr
            xi_ref[b] = xi
            return 0

        lax.fori_loop(0, bs, body, 0)

    ys = []
    for blk in range(width // LANES):
        y = jnp.zeros((rows, LANES), F32)
        for c in range(blk * per_block, (blk + 1) * per_block):
            x2 = jnp.concatenate([bur_sc[c * pitch:c * pitch + rows, :],
                                  bui_sc[c * pitch:c * pitch + rows, :]], axis=1).astype(BF16)
            y = y + _dot(x2, cz_ref[c])
        ys.append(y)
    y = jnp.concatenate(ys, axis=1) + d_ref[...] * s
    zg = jax.nn.gelu(y)
    gate = jax.nn.sigmoid(_dot(zg.astype(BF16), gw_ref[...]) + gb_ref[...])
    o_ref[...] = (zg * gate).astype(o_ref.dtype).reshape(o_ref.shape)


def _ssm_tables(lam_re, lam_im, log_dt, b_re, b_im, c_re, c_im):
    g, p = lam_re.shape
    hch = b_re.shape[-1]
    dt = jnp.exp(log_dt)[:, None]
    mag = jnp.exp(lam_re * dt)
    ab_re, ab_im = mag * jnp.cos(lam_im * dt), mag * jnp.sin(lam_im * dt)
    den = lam_re * lam_re + lam_im * lam_im
    nr, ni = ab_re - 1.0, ab_im
    zr, zi = (nr * lam_re + ni * lam_im) / den, (ni * lam_re - nr * lam_im) / den
    bb_re = zr[..., None] * b_re - zi[..., None] * b_im
    bb_im = zr[..., None] * b_im + zi[..., None] * b_re
    n_slab = g * p // SLAB
    gps = SLAB // p
    per_block = LANES // (gps * hch)
    eye_g = jnp.eye(gps, dtype=F32)
    eye_b = jnp.eye(per_block, dtype=F32)

    def in_mat(bb):
        x = bb.reshape(n_slab, gps, p, hch)
        m = jnp.einsum('cepz,ef->cezfp', x, eye_g).reshape(n_slab, gps * hch, SLAB)
        sel = eye_b[jnp.arange(n_slab) % per_block]
        return jnp.einsum('crn,cq->cqrn', m, sel).reshape(n_slab, LANES, SLAB)

    def out_mat(cc):
        x = cc.reshape(n_slab, gps, hch, p)
        m = jnp.einsum('cezp,ef->cepfz', x, eye_g).reshape(n_slab, SLAB, gps * hch)
        sel = eye_b[jnp.arange(n_slab) % per_block]
        return jnp.einsum('cnr,cq->cnqr', m, sel).reshape(n_slab, SLAB, LANES)

    bz = jnp.concatenate([in_mat(bb_re), in_mat(bb_im)], axis=2).astype(BF16)
    cz = jnp.concatenate([out_mat(c_re), -out_mat(c_im)], axis=1).astype(BF16)
    return ab_re.reshape(n_slab, SLAB), ab_im.reshape(n_slab, SLAB), bz, cz


def _ssm(s, x0r, x0i, tables, d_skip, glu_w, glu_b, *, bs, tt, time_outer, out_dtype):
    nseq, length, width = s.shape
    ar, ai, bz, cz = tables
    n_slab = ar.shape[0]
    rows = bs * tt
    pitch = rows + 8
    if time_outer:
        assert bs == nseq
        grid = (length // tt,)
        smap = lambda i: (0, i, 0)
        xmap = lambda i: (0, 0, 0)
    else:
        assert tt == length
        grid = (nseq // bs,)
        smap = lambda i: (i, 0, 0)
        xmap = lambda i: (i, 0, 0)
    const2 = lambda shape: pl.BlockSpec(shape, lambda i: (0, 0))
    const3 = lambda shape: pl.BlockSpec(shape, lambda i: (0, 0, 0))
    xspec = pl.BlockSpec((bs, n_slab, SLAB), xmap)
    return pl.pallas_call(
        functools.partial(_ssm_kernel, bs=bs, tt=tt, time_outer=time_outer),
        out_shape=(jax.ShapeDtypeStruct((nseq, length, width), out_dtype),
                   jax.ShapeDtypeStruct((nseq, n_slab, SLAB), F32),
                   jax.ShapeDtypeStruct((nseq, n_slab, SLAB), F32)),
        grid=grid,
        in_specs=[pl.BlockSpec((bs, tt, width), smap), xspec, xspec,
                  const2((n_slab, SLAB)), const2((n_slab, SLAB)),
                  const3(bz.shape), const3(cz.shape),
                  const2((1, width)), const2(glu_w.shape), const2((1, width))],
        out_specs=(pl.BlockSpec((bs, tt, width), smap), xspec, xspec),
        scratch_shapes=[pltpu.VMEM((n_slab * pitch, SLAB), F32),
                        pltpu.VMEM((n_slab * pitch, SLAB), F32)],
        compiler_params=_params("arbitrary"),
        name="ssm",
    )(s, x0r, x0i, ar, ai, bz, cz, d_skip.reshape(1, width), glu_w, glu_b.reshape(1, width))


def _mix_kernel(oa_ref, ob_ref, ga_ref, gb_ref, h_ref, g2_ref, wa_ref, wb_ref, wo_ref, o_ref):
    merged = (jax.nn.sigmoid(ga_ref[...]) * _dot(oa_ref[...].astype(BF16), wa_ref[...])
              + jax.nn.sigmoid(gb_ref[...]) * _dot(ob_ref[...].astype(BF16), wb_ref[...]))
    mix = _dot(merged.astype(BF16), wo_ref[...])
    o_ref[...] = h_ref[...] + g2_ref[...] * mix


def _mix(oa, ob, ga, gb, h, mod, per_row, rows_per_seq, wa, wb, wo, *, tm):
    t, d = h.shape
    w = oa.shape[1]
    row = lambda width: pl.BlockSpec((tm, width), lambda i: (i, 0))
    full = lambda a: pl.BlockSpec(a.shape, lambda i: (0, 0))
    return pl.pallas_call(
        _mix_kernel,
        out_shape=jax.ShapeDtypeStruct((t, d), F32),
        grid=(t // tm,),
        in_specs=[row(w), row(w), row(d), row(d), row(d),
                  _mod_spec(per_row, tm, rows_per_seq, 5, d),
                  full(wa), full(wb), full(wo)],
        out_specs=row(d),
        compiler_params=_params("parallel"),
        name="mix",
    )(oa, ob, ga, gb, h, mod, wa, wb, wo)


def kernel(x_prompt, x_sample, c_prompt, c_sample, cache_k, cache_v, state_ssm_re, state_ssm_im, page_table, ada_w, ada_b, norm_ffn1, ffn1_w_in, ffn1_w_out, norm_mix, w_in, sb_bias, ssm_lambda_re, ssm_lambda_im, ssm_log_dt, ssm_b_re, ssm_b_im, ssm_c_re, ssm_c_im, ssm_d, glu_w, glu_b, w_branch_a, w_branch_b, w_out, norm_ffn2, ffn2_w_in, ffn2_w_out, final_norm):
    assert ada_w.shape[0] == 1, "single-layer step"
    sq = lambda a: a.reshape(a.shape[1:])
    (ada_w, ada_b, norm_ffn1, ffn1_w_in, ffn1_w_out, norm_mix, w_in, sb_bias, ssm_lambda_re, ssm_lambda_im,
     ssm_log_dt, ssm_b_re, ssm_b_im, ssm_c_re, ssm_c_im, ssm_d, glu_w, glu_b, w_branch_a, w_branch_b, w_out,
     norm_ffn2, ffn2_w_in, ffn2_w_out, cache_k, cache_v, state_ssm_re, state_ssm_im) = map(sq, (
         ada_w, ada_b, norm_ffn1, ffn1_w_in, ffn1_w_out, norm_mix, w_in, sb_bias, ssm_lambda_re, ssm_lambda_im,
         ssm_log_dt, ssm_b_re, ssm_b_im, ssm_c_re, ssm_c_im, ssm_d, glu_w, glu_b, w_branch_a, w_branch_b, w_out,
         norm_ffn2, ffn2_w_in, ffn2_w_out, cache_k, cache_v, state_ssm_re, state_ssm_im))
    bp, sp, d = x_prompt.shape
    bsm, ssm_len, _ = x_sample.shape
    sbw = N_HEADS * HEAD_DIM
    n_pool, page = cache_k.shape[0], cache_k.shape[1]
    n_slab = ssm_lambda_re.shape[0] * ssm_lambda_re.shape[1] // SLAB

    n_c = bp + bsm
    pad_c = (-n_c) % 8
    c_all = jnp.concatenate([c_prompt, c_sample, jnp.zeros((pad_c, d), F32)], axis=0)
    mod = _ada(c_all, ada_w, ada_b)
    mod_p = mod[:bp].reshape(bp, 1, -1)
    mod_s = jnp.repeat(mod[bp:n_c], ssm_len, axis=0)

    bf = lambda a: a.astype(BF16)
    w1i, w1o, w2i, w2o = bf(ffn1_w_in), bf(ffn1_w_out), bf(ffn2_w_in), bf(ffn2_w_out)
    wi, wa, wb, wo, gw = bf(w_in), bf(w_branch_a), bf(w_branch_b), bf(w_out), bf(glu_w)
    row = lambda a: a.reshape(1, -1)
    tables = _ssm_tables(ssm_lambda_re, ssm_lambda_im, ssm_log_dt, ssm_b_re, ssm_b_im, ssm_c_re, ssm_c_im)
    cache_k2 = bf(cache_k).reshape(n_pool, page, sbw)
    cache_v2 = bf(cache_v).reshape(n_pool, page, sbw)

    outs = {}
    for name, x, m, per_row, nseq, length in (("p", x_prompt, mod_p, False, bp, sp),
                                              ("s", x_sample, mod_s, True, bsm, ssm_len)):
        t = nseq * length
        tm = _largest_tile(t, 512, 8)
        tm_in = _largest_tile(t, 256, 8)
        x2 = x.reshape(t, d)
        h1 = _ffn(x2, m, per_row, length, 0, row(norm_ffn1), w1i, w1o, row(final_norm), final=False, tm=tm)
        q, k, v, kb, vb, s, ga, gb = _inproj(h1, m, per_row, length, row(norm_mix), wi, tm=tm_in)
        s3 = s.reshape(nseq, length, -1)
        seq3 = lambda a: a.reshape(nseq, length, sbw)
        if name == "p":
            oa = _attn_prompt(seq3(q), seq3(kb), seq3(vb), sb_bias, tq=min(256, length), pairs=4)
            zero = jnp.zeros((nseq, n_slab, SLAB), F32)
            ob, xr, xi = _ssm(s3, zero, zero, tables, ssm_d, gw, glu_b,
                              bs=nseq, tt=min(64, length), time_outer=True, out_dtype=BF16)
        else:
            oa = _attn_sample(seq3(q).astype(F32), seq3(kb), seq3(vb), cache_k2, cache_v2, page_table, sb_bias)
            ob, xr, xi = _ssm(s3, state_ssm_re.reshape(nseq, n_slab, SLAB),
                              state_ssm_im.reshape(nseq, n_slab, SLAB), tables, ssm_d, gw, glu_b,
                              bs=min(16, nseq), tt=length, time_outer=False, out_dtype=F32)
        h2 = _mix(oa.reshape(t, sbw), ob.reshape(t, -1), ga, gb, h1, m, per_row, length, wa, wb, wo, tm=tm)
        y = _ffn(h2, m, per_row, length, 6, row(norm_ffn2), w2i, w2o, row(final_norm), final=True, tm=tm)
        st_shape = (1, nseq) + ssm_lambda_re.shape
        outs[name] = (y.reshape(nseq, length, d),
                      k.reshape(1, nseq, length, N_HEADS, HEAD_DIM),
                      v.reshape(1, nseq, length, N_HEADS, HEAD_DIM),
                      xr.reshape(st_shape), xi.reshape(st_shape))
    yp, kp, vp, srp, sip = outs["p"]
    ys, ks, vs, srs, sis = outs["s"]
    return (yp, ys, kp, vp, srp, sip, ks, vs, srs, sis)
```

```python
import functools

import jax
import jax.numpy as jnp
from jax import lax
from jax.experimental import pallas as pl
from jax.experimental.pallas import tpu as pltpu

F32 = jnp.float32
BF16 = jnp.bfloat16

RMS_EPS = 1e-6
N_HEADS = 8
HEAD_DIM = 64
LANES = 128
SSM_GROUP = 16
SSM_STATE = 64
SLAB = LANES
VMEM_LIMIT = 56 * 1024 * 1024


def _params(*sem):
    return pltpu.CompilerParams(dimension_semantics=sem, vmem_limit_bytes=VMEM_LIMIT)


def _dot(a, b):
    return jnp.dot(a, b, preferred_element_type=F32)


def _dot_nt(a, b):
    return lax.dot_general(a, b, (((1,), (1,)), ((), ())), preferred_element_type=F32)


def _rms(x):
    return x * lax.rsqrt(jnp.mean(x * x, axis=-1, keepdims=True) + RMS_EPS)


def _rms_mod(x, gain, shift, scale):
    return _rms(x) * gain * (1.0 + scale) + shift


def _mod_spec(per_row, tm, rows_per_seq, chunk, d):
    if per_row:
        return pl.BlockSpec((tm, d), lambda i, *_: (i, chunk))
    return pl.BlockSpec((None, 1, d), lambda i, *_: (i * tm // rows_per_seq, 0, chunk))


def _largest_tile(n, cap, mult):
    best = None
    for t in range(mult, min(n, cap) + 1, mult):
        if n % t == 0:
            best = t
    assert best is not None, (n, cap, mult)
    return best


def _ada_kernel(c_ref, w_ref, b_ref, o_ref):
    c = c_ref[...]
    a = (c * jax.nn.sigmoid(c)).astype(BF16)
    o_ref[...] = _dot(a, w_ref[...].astype(BF16)) + b_ref[...]


def _ada(c, ada_w, ada_b):
    rows, d = c.shape
    n = ada_w.shape[1]
    tn = _largest_tile(n, 1536, LANES)
    return pl.pallas_call(
        _ada_kernel,
        out_shape=jax.ShapeDtypeStruct((rows, n), F32),
        grid=(n // tn,),
        in_specs=[pl.BlockSpec((rows, d), lambda j: (0, 0)),
                  pl.BlockSpec((d, tn), lambda j: (0, j)),
                  pl.BlockSpec((1, tn), lambda j: (0, j))],
        out_specs=pl.BlockSpec((rows, tn), lambda j: (0, j)),
        compiler_params=_params("arbitrary"),
        name="ada",
    )(c, ada_w, ada_b.reshape(1, n))


def _ffn_kernel(x_ref, sh_ref, sc_ref, g_ref, gain_ref, wg_ref, wu_ref, wo_ref, fin_ref, o_ref,
                u_sc, acc_sc, *, final):
    j = pl.program_id(1)
    nj = pl.num_programs(1)

    @pl.when(j == 0)
    def _():
        u_sc[...] = _rms_mod(x_ref[...], gain_ref[...], sh_ref[...], sc_ref[...]).astype(BF16)

    u = u_sc[...]
    hg = _dot(u, wg_ref[...])
    hu = _dot(u, wu_ref[...])
    act = (hg * jax.nn.sigmoid(hg) * hu).astype(BF16)
    part = _dot(act, wo_ref[...])

    @pl.when(j == 0)
    def _():
        acc_sc[...] = part

    @pl.when(j > 0)
    def _():
        acc_sc[...] += part

    @pl.when(j == nj - 1)
    def _():
        h = x_ref[...] + 0.5 * g_ref[...] * acc_sc[...]
        if final:
            h = _rms(h) * fin_ref[...]
        o_ref[...] = h


def _ffn(x, mod, per_row, rows_per_seq, chunk0, gain, w_in, w_out, fin, *, final, tm):
    t, d = x.shape
    f = w_out.shape[0]
    tf = _largest_tile(f, 1408, LANES)
    nj = f // tf
    mspec = functools.partial(_mod_spec, per_row, tm, rows_per_seq, d=d)
    return pl.pallas_call(
        functools.partial(_ffn_kernel, final=final),
        out_shape=jax.ShapeDtypeStruct((t, d), F32),
        grid=(t // tm, nj),
        in_specs=[pl.BlockSpec((tm, d), lambda i, j: (i, 0)),
                  mspec(chunk0), mspec(chunk0 + 1), mspec(chunk0 + 2),
                  pl.BlockSpec((1, d), lambda i, j: (0, 0)),
                  pl.BlockSpec((d, tf), lambda i, j: (0, j)),
                  pl.BlockSpec((d, tf), lambda i, j: (0, nj + j)),
                  pl.BlockSpec((tf, d), lambda i, j: (j, 0)),
                  pl.BlockSpec((1, d), lambda i, j: (0, 0))],
        out_specs=pl.BlockSpec((tm, d), lambda i, j: (i, 0)),
        scratch_shapes=[pltpu.VMEM((tm, d), BF16), pltpu.VMEM((tm, d), F32)],
        compiler_params=_params("parallel", "arbitrary"),
        name="ffn_final" if final else "ffn",
    )(x, mod, mod, mod, gain, w_in, w_in, w_out, fin)


def _inproj_kernel(h_ref, sh_ref, sc_ref, gain_ref, w_ref,
                   q_ref, k_ref, v_ref, kb_ref, vb_ref, s_ref, ga_ref, gb_ref, *, sbw, ssw, d):
    u = _rms_mod(h_ref[...], gain_ref[...], sh_ref[...], sc_ref[...]).astype(BF16)
    o = 0
    q = _dot(u, w_ref[:, o:o + sbw]); o += sbw
    k = _dot(u, w_ref[:, o:o + sbw]); o += sbw
    v = _dot(u, w_ref[:, o:o + sbw]); o += sbw
    q_ref[...] = (q * (HEAD_DIM ** -0.5)).astype(BF16)
    k_ref[...] = k
    v_ref[...] = v
    kb_ref[...] = k.astype(BF16)
    vb_ref[...] = v.astype(BF16)
    s_ref[...] = _dot(u, w_ref[:, o:o + ssw]); o += ssw
    ga_ref[...] = _dot(u, w_ref[:, o:o + d]); o += d
    gb_ref[...] = _dot(u, w_ref[:, o:o + d])


def _inproj(h, mod, per_row, rows_per_seq, gain, w, *, tm):
    t, d = h.shape
    sbw = N_HEADS * HEAD_DIM
    ssw = d // 2
    n = w.shape[1]
    assert n == 3 * sbw + ssw + 2 * d
    mspec = functools.partial(_mod_spec, per_row, tm, rows_per_seq, d=d)
    row = lambda width: pl.BlockSpec((tm, width), lambda i: (i, 0))
    sds = lambda width, dt: jax.ShapeDtypeStruct((t, width), dt)
    return pl.pallas_call(
        functools.partial(_inproj_kernel, sbw=sbw, ssw=ssw, d=d),
        out_shape=(sds(sbw, BF16), sds(sbw, F32), sds(sbw, F32), sds(sbw, BF16), sds(sbw, BF16),
                   sds(ssw, F32), sds(d, F32), sds(d, F32)),
        grid=(t // tm,),
        in_specs=[row(d), mspec(3), mspec(4),
                  pl.BlockSpec((1, d), lambda i: (0, 0)),
                  pl.BlockSpec((d, n), lambda i: (0, 0))],
        out_specs=(row(sbw), row(sbw), row(sbw), row(sbw), row(sbw), row(ssw), row(d), row(d)),
        compiler_params=_params("parallel"),
        name="inproj",
    )(h, mod, mod, gain, w)


def _sb_drop(z, cs2, mask):
    sp = jnp.maximum(z, 0.0) + jnp.log(1.0 + jnp.exp(-jnp.abs(z)))
    log_beta = z - sp
    if mask is not None:
        sp = jnp.where(mask, sp, 0.0)
    hi = sp.astype(BF16)
    lo = (sp - hi.astype(F32)).astype(BF16)
    drop_in = _dot(jnp.concatenate([hi, lo], axis=1), cs2)
    return log_beta, drop_in, drop_in[:, 0:1] + sp[:, 0:1]


def _sb_weights(log_beta, drop_in, drop_later, mask):
    w = jnp.exp(log_beta - drop_in - drop_later)
    if mask is not None:
        w = jnp.where(mask, w, 0.0)
    return w


def _cs_matrix(n):
    r = lax.broadcasted_iota(jnp.int32, (2 * n, n), 0) % n
    c = lax.broadcasted_iota(jnp.int32, (2 * n, n), 1)
    return jnp.where(r > c, 1.0, 0.0).astype(BF16)


def _attn_prompt_kernel(bias_ref, q_ref, k_ref, v_ref, o_ref, *, tq, pairs):
    g = pl.program_id(1)
    i = pl.program_id(2)
    lane = lax.broadcasted_iota(jnp.int32, (tq, LANES), 1)
    cs2 = _cs_matrix(tq)
    r = lax.broadcasted_iota(jnp.int32, (tq, tq), 0)
    c = lax.broadcasted_iota(jnp.int32, (tq, tq), 1)
    diag_mask = c < r
    heads = []
    for pp in range(pairs):
        q2 = q_ref[:, pp * LANES:(pp + 1) * LANES]
        for hh in range(2):
            in_head = (lane >= hh * HEAD_DIM) & (lane < (hh + 1) * HEAD_DIM)
            heads.append((slice(pp * LANES, (pp + 1) * LANES), jnp.where(in_head, q2, jnp.zeros_like(q2)),
                          bias_ref[(g * pairs + pp) * 2 + hh]))

    def tiles(kb, state, mask):
        rows = pl.ds(pl.multiple_of(kb * tq, tq), tq)
        n = len(heads)
        zs, drops, out = {}, {}, []
        for step in range(n + 2):
            if step < n:
                cols, qm, bias = heads[step]
                zs[step] = _dot_nt(qm, k_ref[rows, cols]) + bias
            if 1 <= step <= n:
                drops[step - 1] = _sb_drop(zs.pop(step - 1), cs2, mask)
            if step >= 2:
                h = step - 2
                log_beta, drop_in, drop_tot = drops.pop(h)
                drop_later, acc = state[h]
                w = _sb_weights(log_beta, drop_in, drop_later, mask)
                out.append((drop_later + drop_tot, acc + _dot(w.astype(BF16), v_ref[rows, heads[h][0]])))
        return tuple(out)

    zero = (jnp.zeros((tq, 1), F32), jnp.zeros((tq, LANES), F32))
    state = tiles(i, (zero,) * len(heads), diag_mask)
    state = lax.fori_loop(0, i, lambda n, st: tiles(i - 1 - n, st, None), state)
    for pp in range(pairs):
        o_ref[:, pp * LANES:(pp + 1) * LANES] = jnp.where(
            lane < HEAD_DIM, state[2 * pp][1], state[2 * pp + 1][1]).astype(o_ref.dtype)


def _attn_prompt(q, k, v, bias, *, tq, pairs):
    b, s, w = q.shape
    wb = pairs * LANES
    return pl.pallas_call(
        functools.partial(_attn_prompt_kernel, tq=tq, pairs=pairs),
        out_shape=jax.ShapeDtypeStruct((b, s, w), BF16),
        grid=(b, w // wb, s // tq),
        in_specs=[pl.BlockSpec(memory_space=pltpu.SMEM),
                  pl.BlockSpec((None, tq, wb), lambda bi, g, i: (bi, i, g)),
                  pl.BlockSpec((None, s, wb), lambda bi, g, i: (bi, 0, g)),
                  pl.BlockSpec((None, s, wb), lambda bi, g, i: (bi, 0, g))],
        out_specs=pl.BlockSpec((None, tq, wb), lambda bi, g, i: (bi, i, g)),
        compiler_params=_params("parallel", "parallel", "arbitrary"),
        name="attn_prompt",
    )(bias, q, k, v)


def _attn_sample_kernel(pt_ref, bias_ref, q_ref, kn_ref, vn_ref, *rest, t_new, page, pg):
    kc_refs, vc_refs = rest[:pg], rest[pg:2 * pg]
    o_ref, qbd_sc, carry_sc, acc_sc = rest[2 * pg:]
    j = pl.program_id(1)
    nj = pl.num_programs(1)
    rows = N_HEADS * t_new
    width = N_HEADS * HEAD_DIM
    lane = lax.broadcasted_iota(jnp.int32, (t_new, width), 1)
    cs2 = _cs_matrix(page)
    bias_col = jnp.concatenate(
        [jnp.full((t_new, 1), bias_ref[h], F32) for h in range(N_HEADS)], axis=0)

    def tiles(blocks, mask):
        q = qbd_sc[...].astype(BF16)
        n = len(blocks)
        zs, drops = {}, {}
        carry, acc = carry_sc[...], acc_sc[...]
        for step in range(n + 2):
            if step < n:
                zs[step] = _dot_nt(q, blocks[step][0][...]) + bias_col
            if 1 <= step <= n:
                drops[step - 1] = _sb_drop(zs.pop(step - 1), cs2, mask)
            if step >= 2:
                log_beta, drop_in, drop_tot = drops.pop(step - 2)
                w = _sb_weights(log_beta, drop_in, carry, mask)
                acc = acc + _dot(w.astype(BF16), blocks[step - 2][1][...])
                carry = carry + drop_tot
        carry_sc[...] = carry
        acc_sc[...] = acc

    @pl.when(j == 0)
    def _():
        q = q_ref[...]
        for h in range(N_HEADS):
            in_head = (lane >= h * HEAD_DIM) & (lane < (h + 1) * HEAD_DIM)
            qbd_sc[h * t_new:(h + 1) * t_new, :] = jnp.where(in_head, q, jnp.zeros_like(q))
        carry_sc[...] = jnp.zeros_like(carry_sc)
        acc_sc[...] = jnp.zeros_like(acc_sc)
        rq = lax.broadcasted_iota(jnp.int32, (rows, page), 0) % t_new
        ck = lax.broadcasted_iota(jnp.int32, (rows, page), 1)
        tiles([(kn_ref, vn_ref)], ck < rq)

    tiles(list(zip(kc_refs, vc_refs)), None)

    @pl.when(j == nj - 1)
    def _():
        acc = acc_sc[...]
        out = jnp.zeros((t_new, width), F32)
        for h in range(N_HEADS):
            in_head = (lane >= h * HEAD_DIM) & (lane < (h + 1) * HEAD_DIM)
            out = out + jnp.where(in_head, acc[h * t_new:(h + 1) * t_new, :], 0.0)
        o_ref[...] = out.astype(o_ref.dtype)


def _attn_sample(q, k_new, v_new, cache_k, cache_v, page_table, bias):
    nb, t_new, width = q.shape
    n_pages = page_table.shape[1]
    page = cache_k.shape[1]
    rows = N_HEADS * t_new
    pad = ((0, 0), (0, page - t_new), (0, 0))
    k_new = jnp.pad(k_new, pad)
    v_new = jnp.pad(v_new, pad)
    pg = _largest_tile(n_pages, 4, 1)
    seq = lambda rws: pl.BlockSpec((None, rws, width), lambda b, j, pt: (b, 0, 0))
    cache = [pl.BlockSpec((None, page, width),
                          lambda b, j, pt, u=u: (pt[b, n_pages - 1 - (j * pg + u)], 0, 0)) for u in range(pg)]
    return pl.pallas_call(
        functools.partial(_attn_sample_kernel, t_new=t_new, page=page, pg=pg),
        out_shape=jax.ShapeDtypeStruct((nb, t_new, width), F32),
        grid_spec=pltpu.PrefetchScalarGridSpec(
            num_scalar_prefetch=1,
            grid=(nb, n_pages // pg),
            in_specs=[pl.BlockSpec(memory_space=pltpu.SMEM),
                      seq(t_new), seq(page), seq(page)] + cache + cache,
            out_specs=seq(t_new),
            scratch_shapes=[pltpu.VMEM((rows, width), F32),
                            pltpu.VMEM((rows, 1), F32),
                            pltpu.VMEM((rows, width), F32)]),
        compiler_params=_params("parallel", "arbitrary"),
        name="attn_sample",
    )(page_table, bias, q, k_new, v_new, *([cache_k] * pg), *([cache_v] * pg))


def _ssm_kernel(s_ref, x0r_ref, x0i_ref, ar_ref, ai_ref, bz_ref, cz_ref, d_ref, gw_ref, gb_ref,
                o_ref, xr_ref, xi_ref, bur_sc, bui_sc, *, bs, tt, time_outer):
    step = pl.program_id(0)
    rows = bs * tt
    pitch = rows + 8
    n_slab = ar_ref.shape[0]
    width = s_ref.shape[-1]

    def init():
        xr_ref[...] = x0r_ref[...]
        xi_ref[...] = x0i_ref[...]

    if time_outer:
        pl.when(step == 0)(init)
    else:
        init()

    s = s_ref[...].reshape(rows, width)
    sb = s.astype(BF16)
    per_block = LANES // (2 * SSM_GROUP)
    for c in range(n_slab):
        blk = c // per_block
        bu = _dot(sb[:, blk * LANES:(blk + 1) * LANES], bz_ref[c])
        bur_sc[c * pitch:c * pitch + rows, :] = bu[:, :SLAB]
        bui_sc[c * pitch:c * pitch + rows, :] = bu[:, SLAB:]

    ar = ar_ref[...]
    ai = ai_ref[...]

    def one(row, xr, xi):
        idx = pl.ds(row, n_slab, stride=pitch)
        nr = ar * xr - ai * xi + bur_sc[idx, :]
        ni = ar * xi + ai * xr + bui_sc[idx, :]
        bur_sc[idx, :] = nr
        bui_sc[idx, :] = ni
        return nr, ni

    if time_outer:
        def body(t, st):
            out = []
            for b in range(bs):
                out.append(one(b * tt + t, st[b][0], st[b][1]))
            return tuple(out)

        st = lax.fori_loop(0, tt, body, tuple((xr_ref[b], xi_ref[b]) for b in range(bs)))
        for b in range(bs):
            xr_ref[b] = st[b][0]
            xi_ref[b] = st[b][1]
    else:
        def body(b, _):
            xr, xi = xr_ref[b], xi_ref[b]
            for t in range(tt):
                xr, xi = one(b * tt + t, xr, xi)
            xr_ref[b] = xr
            xi_ref[b] = xi
            return 0

        lax.fori_loop(0, bs, body, 0)

    ys = []
    for blk in range(width // LANES):
        y = jnp.zeros((rows, LANES), F32)
        for c in range(blk * per_block, (blk + 1) * per_block):
            x2 = jnp.concatenate([bur_sc[c * pitch:c * pitch + rows, :],
                                  bui_sc[c * pitch:c * pitch + rows, :]], axis=1).astype(BF16)
            y = y + _dot(x2, cz_ref[c])
        ys.append(y)
    y = jnp.concatenate(ys, axis=1) + d_ref[...] * s
    zg = jax.nn.gelu(y)
    gate = jax.nn.sigmoid(_dot(zg.astype(BF16), gw_ref[...]) + gb_ref[...])
    o_ref[...] = (zg * gate).astype(o_ref.dtype).reshape(o_ref.shape)


def _ssm_tables(lam_re, lam_im, log_dt, b_re, b_im, c_re, c_im):
    g, p = lam_re.shape
    hch = b_re.shape[-1]
    dt = jnp.exp(log_dt)[:, None]
    mag = jnp.exp(lam_re * dt)
    ab_re, ab_im = mag * jnp.cos(lam_im * dt), mag * jnp.sin(lam_im * dt)
    den = lam_re * lam_re + lam_im * lam_im
    nr, ni = ab_re - 1.0, ab_im
    zr, zi = (nr * lam_re + ni * lam_im) / den, (ni * lam_re - nr * lam_im) / den
    bb_re = zr[..., None] * b_re - zi[..., None] * b_im
    bb_im = zr[..., None] * b_im + zi[..., None] * b_re
    n_slab = g * p // SLAB
    gps = SLAB // p
    per_block = LANES // (gps * hch)
    eye_g = jnp.eye(gps, dtype=F32)
    eye_b = jnp.eye(per_block, dtype=F32)

    def in_mat(bb):
        x = bb.reshape(n_slab, gps, p, hch)
        m = jnp.einsum('cepz,ef->cezfp', x, eye_g).reshape(n_slab, gps * hch, SLAB)
        sel = eye_b[jnp.arange(n_slab) % per_block]
        return jnp.einsum('crn,cq->cqrn', m, sel).reshape(n_slab, LANES, SLAB)

    def out_mat(cc):
        x = cc.reshape(n_slab, gps, hch, p)
        m = jnp.einsum('cezp,ef->cepfz', x, eye_g).reshape(n_slab, SLAB, gps * hch)
        sel = eye_b[jnp.arange(n_slab) % per_block]
        return jnp.einsum('cnr,cq->cnqr', m, sel).reshape(n_slab, SLAB, LANES)

    bz = jnp.concatenate([in_mat(bb_re), in_mat(bb_im)], axis=2).astype(BF16)
    cz = jnp.concatenate([out_mat(c_re), -out_mat(c_im)], axis=1).astype(BF16)
    return ab_re.reshape(n_slab, SLAB), ab_im.reshape(n_slab, SLAB), bz, cz


def _ssm(s, x0r, x0i, tables, d_skip, glu_w, glu_b, *, bs, tt, time_outer, out_dtype):
    nseq, length, width = s.shape
    ar, ai, bz, cz = tables
    n_slab = ar.shape[0]
    rows = bs * tt
    pitch = rows + 8
    if time_outer:
        assert bs == nseq
        grid = (length // tt,)
        smap = lambda i: (0, i, 0)
        xmap = lambda i: (0, 0, 0)
    else:
        assert tt == length
        grid = (nseq // bs,)
        smap = lambda i: (i, 0, 0)
        xmap = lambda i: (i, 0, 0)
    const2 = lambda shape: pl.BlockSpec(shape, lambda i: (0, 0))
    const3 = lambda shape: pl.BlockSpec(shape, lambda i: (0, 0, 0))
    xspec = pl.BlockSpec((bs, n_slab, SLAB), xmap)
    return pl.pallas_call(
        functools.partial(_ssm_kernel, bs=bs, tt=tt, time_outer=time_outer),
        out_shape=(jax.ShapeDtypeStruct((nseq, length, width), out_dtype),
                   jax.ShapeDtypeStruct((nseq, n_slab, SLAB), F32),
                   jax.ShapeDtypeStruct((nseq, n_slab, SLAB), F32)),
        grid=grid,
        in_specs=[pl.BlockSpec((bs, tt, width), smap), xspec, xspec,
                  const2((n_slab, SLAB)), const2((n_slab, SLAB)),
                  const3(bz.shape), const3(cz.shape),
                  const2((1, width)), const2(glu_w.shape), const2((1, width))],
        out_specs=(pl.BlockSpec((bs, tt, width), smap), xspec, xspec),
        scratch_shapes=[pltpu.VMEM((n_slab * pitch, SLAB), F32),
                        pltpu.VMEM((n_slab * pitch, SLAB), F32)],
        compiler_params=_params("arbitrary"),
        name="ssm",
    )(s, x0r, x0i, ar, ai, bz, cz, d_skip.reshape(1, width), glu_w, glu_b.reshape(1, width))


def _mix_kernel(oa_ref, ob_ref, ga_ref, gb_ref, h_ref, g2_ref, wa_ref, wb_ref, wo_ref, o_ref):
    merged = (jax.nn.sigmoid(ga_ref[...]) * _dot(oa_ref[...].astype(BF16), wa_ref[...])
              + jax.nn.sigmoid(gb_ref[...]) * _dot(ob_ref[...].astype(BF16), wb_ref[...]))
    mix = _dot(merged.astype(BF16), wo_ref[...])
    o_ref[...] = h_ref[...] + g2_ref[...] * mix


def _mix(oa, ob, ga, gb, h, mod, per_row, rows_per_seq, wa, wb, wo, *, tm):
    t, d = h.shape
    w = oa.shape[1]
    row = lambda width: pl.BlockSpec((tm, width), lambda i: (i, 0))
    full = lambda a: pl.BlockSpec(a.shape, lambda i: (0, 0))
    return pl.pallas_call(
        _mix_kernel,
        out_shape=jax.ShapeDtypeStruct((t, d), F32),
        grid=(t // tm,),
        in_specs=[row(w), row(w), row(d), row(d), row(d),
                  _mod_spec(per_row, tm, rows_per_seq, 5, d),
                  full(wa), full(wb), full(wo)],
        out_specs=row(d),
        compiler_params=_params("parallel"),
        name="mix",
    )(oa, ob, ga, gb, h, mod, wa, wb, wo)


def kernel(x_prompt, x_sample, c_prompt, c_sample, cache_k, cache_v, state_ssm_re, state_ssm_im, page_table, ada_w, ada_b, norm_ffn1, ffn1_w_in, ffn1_w_out, norm_mix, w_in, sb_bias, ssm_lambda_re, ssm_lambda_im, ssm_log_dt, ssm_b_re, ssm_b_im, ssm_c_re, ssm_c_im, ssm_d, glu_w, glu_b, w_branch_a, w_branch_b, w_out, norm_ffn2, ffn2_w_in, ffn2_w_out, final_norm):
    assert ada_w.shape[0] == 1, "single-layer step"
    sq = lambda a: a.reshape(a.shape[1:])
    (ada_w, ada_b, norm_ffn1, ffn1_w_in, ffn1_w_out, norm_mix, w_in, sb_bias, ssm_lambda_re, ssm_lambda_im,
     ssm_log_dt, ssm_b_re, ssm_b_im, ssm_c_re, ssm_c_im, ssm_d, glu_w, glu_b, w_branch_a, w_branch_b, w_out,
     norm_ffn2, ffn2_w_in, ffn2_w_out, cache_k, cache_v, state_ssm_re, state_ssm_im) = map(sq, (
         ada_w, ada_b, norm_ffn1, ffn1_w_in, ffn1_w_out, norm_mix, w_in, sb_bias, ssm_lambda_re, ssm_lambda_im,
         ssm_log_dt, ssm_b_re, ssm_b_im, ssm_c_re, ssm_c_im, ssm_d, glu_w, glu_b, w_branch_a, w_branch_b, w_out,
         norm_ffn2, ffn2_w_in, ffn2_w_out, cache_k, cache_v, state_ssm_re, state_ssm_im))
    bp, sp, d = x_prompt.shape
    bsm, ssm_len, _ = x_sample.shape
    sbw = N_HEADS * HEAD_DIM
    n_pool, page = cache_k.shape[0], cache_k.shape[1]
    n_slab = ssm_lambda_re.shape[0] * ssm_lambda_re.shape[1] // SLAB

    n_c = bp + bsm
    pad_c = (-n_c) % 8
    c_all = jnp.concatenate([c_prompt, c_sample, jnp.zeros((pad_c, d), F32)], axis=0)
    mod = _ada(c_all, ada_w, ada_b)
    mod_p = mod[:bp].reshape(bp, 1, -1)
    mod_s = jnp.repeat(mod[bp:n_c], ssm_len, axis=0)

    bf = lambda a: a.astype(BF16)
    w1i, w1o, w2i, w2o = bf(ffn1_w_in), bf(ffn1_w_out), bf(ffn2_w_in), bf(ffn2_w_out)
    wi, wa, wb, wo, gw = bf(w_in), bf(w_branch_a), bf(w_branch_b), bf(w_out), bf(glu_w)
    row = lambda a: a.reshape(1, -1)
    tables = _ssm_tables(ssm_lambda_re, ssm_lambda_im, ssm_log_dt, ssm_b_re, ssm_b_im, ssm_c_re, ssm_c_im)
    cache_k2 = bf(cache_k.reshape(n_pool, page, sbw))
    cache_v2 = bf(cache_v.reshape(n_pool, page, sbw))

    outs = {}
    for name, x, m, per_row, nseq, length in (("p", x_prompt, mod_p, False, bp, sp),
                                              ("s", x_sample, mod_s, True, bsm, ssm_len)):
        t = nseq * length
        tm = _largest_tile(t, 512, 8)
        tm_in = _largest_tile(t, 256, 8)
        x2 = x.reshape(t, d)
        h1 = _ffn(x2, m, per_row, length, 0, row(norm_ffn1), w1i, w1o, row(final_norm), final=False, tm=tm)
        q, k, v, kb, vb, s, ga, gb = _inproj(h1, m, per_row, length, row(norm_mix), wi, tm=tm_in)
        s3 = s.reshape(nseq, length, -1)
        seq3 = lambda a: a.reshape(nseq, length, sbw)
        if name == "p":
            oa = _attn_prompt(seq3(q), seq3(kb), seq3(vb), sb_bias, tq=min(256, length), pairs=4)
            zero = jnp.zeros((nseq, n_slab, SLAB), F32)
            ob, xr, xi = _ssm(s3, zero, zero, tables, ssm_d, gw, glu_b,
                              bs=nseq, tt=min(64, length), time_outer=True, out_dtype=BF16)
        else:
            oa = _attn_sample(seq3(q).astype(F32), seq3(kb), seq3(vb), cache_k2, cache_v2, page_table, sb_bias)
            ob, xr, xi = _ssm(s3, state_ssm_re.reshape(nseq, n_slab, SLAB),
                              state_ssm_im.reshape(nseq, n_slab, SLAB), tables, ssm_d, gw, glu_b,
                              bs=min(16, nseq), tt=length, time_outer=False, out_dtype=F32)
        h2 = _mix(oa.reshape(t, sbw), ob.reshape(t, -1), ga, gb, h1, m, per_row, length, wa, wb, wo, tm=tm)
        y = _ffn(h2, m, per_row, length, 6, row(norm_ffn2), w2i, w2o, row(final_norm), final=True, tm=tm)
        st_shape = (1, nseq) + ssm_lambda_re.shape
        outs[name] = (y.reshape(nseq, length, d),
                      k.reshape(1, nseq, length, N_HEADS, HEAD_DIM),
                      v.reshape(1, nseq, length, N_HEADS, HEAD_DIM),
                      xr.reshape(st_shape), xi.reshape(st_shape))
    yp, kp, vp, srp, sip = outs["p"]
    ys, ks, vs, srs, sis = outs["s"]
    return (yp, ys, kp, vp, srp, sip, ks, vs, srs, sis)
```

```python
import functools

import jax
import jax.numpy as jnp
from jax import lax
from jax.experimental import pallas as pl
from jax.experimental.pallas import tpu as pltpu

F32 = jnp.float32
BF16 = jnp.bfloat16

RMS_EPS = 1e-6
N_HEADS = 8
HEAD_DIM = 64
LANES = 128
SSM_GROUP = 16
SSM_STATE = 64
SLAB = LANES
VMEM_LIMIT = 56 * 1024 * 1024


def _params(*sem):
    return pltpu.CompilerParams(dimension_semantics=sem, vmem_limit_bytes=VMEM_LIMIT)


def _dot(a, b):
    return jnp.dot(a, b, preferred_element_type=F32)


def _dot_nt(a, b):
    return lax.dot_general(a, b, (((1,), (1,)), ((), ())), preferred_element_type=F32)


def _rms(x):
    return x * lax.rsqrt(jnp.mean(x * x, axis=-1, keepdims=True) + RMS_EPS)


def _rms_mod(x, gain, shift, scale):
    return _rms(x) * gain * (1.0 + scale) + shift


def _mod_spec(per_row, tm, rows_per_seq, chunk, d):
    if per_row:
        return pl.BlockSpec((tm, d), lambda i, *_: (i, chunk))
    return pl.BlockSpec((None, 1, d), lambda i, *_: (i * tm // rows_per_seq, 0, chunk))


def _largest_tile(n, cap, mult):
    best = None
    for t in range(mult, min(n, cap) + 1, mult):
        if n % t == 0:
            best = t
    assert best is not None, (n, cap, mult)
    return best


def _ada_kernel(c_ref, w_ref, b_ref, o_ref):
    c = c_ref[...]
    a = (c * jax.nn.sigmoid(c)).astype(BF16)
    o_ref[...] = _dot(a, w_ref[...].astype(BF16)) + b_ref[...]


def _ada(c, ada_w, ada_b):
    rows, d = c.shape
    n = ada_w.shape[1]
    tn = _largest_tile(n, 1536, LANES)
    return pl.pallas_call(
        _ada_kernel,
        out_shape=jax.ShapeDtypeStruct((rows, n), F32),
        grid=(n // tn,),
        in_specs=[pl.BlockSpec((rows, d), lambda j: (0, 0)),
                  pl.BlockSpec((d, tn), lambda j: (0, j)),
                  pl.BlockSpec((1, tn), lambda j: (0, j))],
        out_specs=pl.BlockSpec((rows, tn), lambda j: (0, j)),
        compiler_params=_params("arbitrary"),
        name="ada",
    )(c, ada_w, ada_b.reshape(1, n))


def _ffn_kernel(x_ref, sh_ref, sc_ref, g_ref, gain_ref, wg_ref, wu_ref, wo_ref, fin_ref, o_ref,
                u_sc, acc_sc, *, final):
    j = pl.program_id(1)
    nj = pl.num_programs(1)

    @pl.when(j == 0)
    def _():
        u_sc[...] = _rms_mod(x_ref[...], gain_ref[...], sh_ref[...], sc_ref[...]).astype(BF16)

    u = u_sc[...]
    hg = _dot(u, wg_ref[...])
    hu = _dot(u, wu_ref[...])
    act = (hg * jax.nn.sigmoid(hg) * hu).astype(BF16)
    part = _dot(act, wo_ref[...])

    @pl.when(j == 0)
    def _():
        acc_sc[...] = part

    @pl.when(j > 0)
    def _():
        acc_sc[...] += part

    @pl.when(j == nj - 1)
    def _():
        h = x_ref[...] + 0.5 * g_ref[...] * acc_sc[...]
        if final:
            h = _rms(h) * fin_ref[...]
        o_ref[...] = h


def _ffn(x, mod, per_row, rows_per_seq, chunk0, gain, w_in, w_out, fin, *, final, tm):
    t, d = x.shape
    f = w_out.shape[0]
    tf = _largest_tile(f, 1408, LANES)
    nj = f // tf
    mspec = functools.partial(_mod_spec, per_row, tm, rows_per_seq, d=d)
    return pl.pallas_call(
        functools.partial(_ffn_kernel, final=final),
        out_shape=jax.ShapeDtypeStruct((t, d), F32),
        grid=(t // tm, nj),
        in_specs=[pl.BlockSpec((tm, d), lambda i, j: (i, 0)),
                  mspec(chunk0), mspec(chunk0 + 1), mspec(chunk0 + 2),
                  pl.BlockSpec((1, d), lambda i, j: (0, 0)),
                  pl.BlockSpec((d, tf), lambda i, j: (0, j)),
                  pl.BlockSpec((d, tf), lambda i, j: (0, nj + j)),
                  pl.BlockSpec((tf, d), lambda i, j: (j, 0)),
                  pl.BlockSpec((1, d), lambda i, j: (0, 0))],
        out_specs=pl.BlockSpec((tm, d), lambda i, j: (i, 0)),
        scratch_shapes=[pltpu.VMEM((tm, d), BF16), pltpu.VMEM((tm, d), F32)],
        compiler_params=_params("parallel", "arbitrary"),
        name="ffn_final" if final else "ffn",
    )(x, mod, mod, mod, gain, w_in, w_in, w_out, fin)


def _inproj_kernel(h_ref, sh_ref, sc_ref, gain_ref, w_ref,
                   q_ref, k_ref, v_ref, kb_ref, vb_ref, s_ref, ga_ref, gb_ref, *, sbw, ssw, d):
    u = _rms_mod(h_ref[...], gain_ref[...], sh_ref[...], sc_ref[...]).astype(BF16)
    o = 0
    q = _dot(u, w_ref[:, o:o + sbw]); o += sbw
    k = _dot(u, w_ref[:, o:o + sbw]); o += sbw
    v = _dot(u, w_ref[:, o:o + sbw]); o += sbw
    q_ref[...] = (q * (HEAD_DIM ** -0.5)).astype(BF16)
    k_ref[...] = k
    v_ref[...] = v
    kb_ref[...] = k.astype(BF16)
    vb_ref[...] = v.astype(BF16)
    s_ref[...] = _dot(u, w_ref[:, o:o + ssw]); o += ssw
    ga_ref[...] = _dot(u, w_ref[:, o:o + d]); o += d
    gb_ref[...] = _dot(u, w_ref[:, o:o + d])


def _inproj(h, mod, per_row, rows_per_seq, gain, w, *, tm):
    t, d = h.shape
    sbw = N_HEADS * HEAD_DIM
    ssw = d // 2
    n = w.shape[1]
    assert n == 3 * sbw + ssw + 2 * d
    mspec = functools.partial(_mod_spec, per_row, tm, rows_per_seq, d=d)
    row = lambda width: pl.BlockSpec((tm, width), lambda i: (i, 0))
    sds = lambda width, dt: jax.ShapeDtypeStruct((t, width), dt)
    return pl.pallas_call(
        functools.partial(_inproj_kernel, sbw=sbw, ssw=ssw, d=d),
        out_shape=(sds(sbw, BF16), sds(sbw, F32), sds(sbw, F32), sds(sbw, BF16), sds(sbw, BF16),
                   sds(ssw, F32), sds(d, F32), sds(d, F32)),
        grid=(t // tm,),
        in_specs=[row(d), mspec(3), mspec(4),
                  pl.BlockSpec((1, d), lambda i: (0, 0)),
                  pl.BlockSpec((d, n), lambda i: (0, 0))],
        out_specs=(row(sbw), row(sbw), row(sbw), row(sbw), row(sbw), row(ssw), row(d), row(d)),
        compiler_params=_params("parallel"),
        name="inproj",
    )(h, mod, mod, gain, w)


def _sb_drop(z, cs2, mask):
    sp = jnp.maximum(z, 0.0) + jnp.log(1.0 + jnp.exp(-jnp.abs(z)))
    log_beta = z - sp
    if mask is not None:
        sp = jnp.where(mask, sp, 0.0)
    hi = sp.astype(BF16)
    lo = (sp - hi.astype(F32)).astype(BF16)
    drop_in = _dot(jnp.concatenate([hi, lo], axis=1), cs2)
    return log_beta, drop_in, drop_in[:, 0:1] + sp[:, 0:1]


def _sb_weights(log_beta, drop_in, drop_later, mask):
    w = jnp.exp(log_beta - drop_in - drop_later)
    if mask is not None:
        w = jnp.where(mask, w, 0.0)
    return w


def _cs_matrix(n):
    r = lax.broadcasted_iota(jnp.int32, (2 * n, n), 0) % n
    c = lax.broadcasted_iota(jnp.int32, (2 * n, n), 1)
    return jnp.where(r > c, 1.0, 0.0).astype(BF16)


def _attn_prompt_kernel(bias_ref, q_ref, k_ref, v_ref, o_ref, *, tq, pairs):
    g = pl.program_id(1)
    i = pl.program_id(2)
    lane = lax.broadcasted_iota(jnp.int32, (tq, LANES), 1)
    cs2 = _cs_matrix(tq)
    r = lax.broadcasted_iota(jnp.int32, (tq, tq), 0)
    c = lax.broadcasted_iota(jnp.int32, (tq, tq), 1)
    diag_mask = c < r
    heads = []
    for pp in range(pairs):
        q2 = q_ref[:, pp * LANES:(pp + 1) * LANES]
        for hh in range(2):
            in_head = (lane >= hh * HEAD_DIM) & (lane < (hh + 1) * HEAD_DIM)
            heads.append((slice(pp * LANES, (pp + 1) * LANES), jnp.where(in_head, q2, jnp.zeros_like(q2)),
                          bias_ref[(g * pairs + pp) * 2 + hh]))

    def tiles(kb, state, mask):
        rows = pl.ds(pl.multiple_of(kb * tq, tq), tq)
        n = len(heads)
        zs, drops, out = {}, {}, []
        for step in range(n + 2):
            if step < n:
                cols, qm, bias = heads[step]
                zs[step] = _dot_nt(qm, k_ref[rows, cols]) + bias
            if 1 <= step <= n:
                drops[step - 1] = _sb_drop(zs.pop(step - 1), cs2, mask)
            if step >= 2:
                h = step - 2
                log_beta, drop_in, drop_tot = drops.pop(h)
                drop_later, acc = state[h]
                w = _sb_weights(log_beta, drop_in, drop_later, mask)
                out.append((drop_later + drop_tot, acc + _dot(w.astype(BF16), v_ref[rows, heads[h][0]])))
        return tuple(out)

    zero = (jnp.zeros((tq, 1), F32), jnp.zeros((tq, LANES), F32))
    state = tiles(i, (zero,) * len(heads), diag_mask)
    state = lax.fori_loop(0, i, lambda n, st: tiles(i - 1 - n, st, None), state)
    for pp in range(pairs):
        o_ref[:, pp * LANES:(pp + 1) * LANES] = jnp.where(
            lane < HEAD_DIM, state[2 * pp][1], state[2 * pp + 1][1]).astype(o_ref.dtype)


def _attn_prompt(q, k, v, bias, *, tq, pairs):
    b, s, w = q.shape
    wb = pairs * LANES
    return pl.pallas_call(
        functools.partial(_attn_prompt_kernel, tq=tq, pairs=pairs),
        out_shape=jax.ShapeDtypeStruct((b, s, w), BF16),
        grid=(b, w // wb, s // tq),
        in_specs=[pl.BlockSpec(memory_space=pltpu.SMEM),
                  pl.BlockSpec((None, tq, wb), lambda bi, g, i: (bi, i, g)),
                  pl.BlockSpec((None, s, wb), lambda bi, g, i: (bi, 0, g)),
                  pl.BlockSpec((None, s, wb), lambda bi, g, i: (bi, 0, g))],
        out_specs=pl.BlockSpec((None, tq, wb), lambda bi, g, i: (bi, i, g)),
        compiler_params=_params("parallel", "parallel", "arbitrary"),
        name="attn_prompt",
    )(bias, q, k, v)


def _attn_sample_kernel(pt_ref, bias_ref, q_ref, kn_ref, vn_ref, *rest, t_new, page, pg):
    kc_refs, vc_refs = rest[:pg], rest[pg:2 * pg]
    o_ref, qbd_sc, carry_sc, acc_sc = rest[2 * pg:]
    j = pl.program_id(1)
    nj = pl.num_programs(1)
    rows = N_HEADS * t_new
    width = N_HEADS * HEAD_DIM
    lane = lax.broadcasted_iota(jnp.int32, (t_new, width), 1)
    cs2 = _cs_matrix(page)
    bias_col = jnp.concatenate(
        [jnp.full((t_new, 1), bias_ref[h], F32) for h in range(N_HEADS)], axis=0)

    def tiles(blocks, mask):
        q = qbd_sc[...].astype(BF16)
        n = len(blocks)
        zs, drops = {}, {}
        carry, acc = carry_sc[...], acc_sc[...]
        for step in range(n + 2):
            if step < n:
                zs[step] = _dot_nt(q, blocks[step][0][...].astype(BF16)) + bias_col
            if 1 <= step <= n:
                drops[step - 1] = _sb_drop(zs.pop(step - 1), cs2, mask)
            if step >= 2:
                log_beta, drop_in, drop_tot = drops.pop(step - 2)
                w = _sb_weights(log_beta, drop_in, carry, mask)
                acc = acc + _dot(w.astype(BF16), blocks[step - 2][1][...].astype(BF16))
                carry = carry + drop_tot
        carry_sc[...] = carry
        acc_sc[...] = acc

    @pl.when(j == 0)
    def _():
        q = q_ref[...]
        for h in range(N_HEADS):
            in_head = (lane >= h * HEAD_DIM) & (lane < (h + 1) * HEAD_DIM)
            qbd_sc[h * t_new:(h + 1) * t_new, :] = jnp.where(in_head, q, jnp.zeros_like(q))
        carry_sc[...] = jnp.zeros_like(carry_sc)
        acc_sc[...] = jnp.zeros_like(acc_sc)
        rq = lax.broadcasted_iota(jnp.int32, (rows, page), 0) % t_new
        ck = lax.broadcasted_iota(jnp.int32, (rows, page), 1)
        tiles([(kn_ref, vn_ref)], ck < rq)

    tiles(list(zip(kc_refs, vc_refs)), None)

    @pl.when(j == nj - 1)
    def _():
        acc = acc_sc[...]
        out = jnp.zeros((t_new, width), F32)
        for h in range(N_HEADS):
            in_head = (lane >= h * HEAD_DIM) & (lane < (h + 1) * HEAD_DIM)
            out = out + jnp.where(in_head, acc[h * t_new:(h + 1) * t_new, :], 0.0)
        o_ref[...] = out.astype(o_ref.dtype)


def _attn_sample(q, k_new, v_new, cache_k, cache_v, page_table, bias):
    nb, t_new, width = q.shape
    n_pages = page_table.shape[1]
    page = cache_k.shape[1]
    rows = N_HEADS * t_new
    pad = ((0, 0), (0, page - t_new), (0, 0))
    k_new = jnp.pad(k_new, pad)
    v_new = jnp.pad(v_new, pad)
    pg = _largest_tile(n_pages, 4, 1)
    seq = lambda rws: pl.BlockSpec((None, rws, width), lambda b, j, pt: (b, 0, 0))
    cache = [pl.BlockSpec((None, page, width),
                          lambda b, j, pt, u=u: (pt[b, n_pages - 1 - (j * pg + u)], 0, 0)) for u in range(pg)]
    return pl.pallas_call(
        functools.partial(_attn_sample_kernel, t_new=t_new, page=page, pg=pg),
        out_shape=jax.ShapeDtypeStruct((nb, t_new, width), F32),
        grid_spec=pltpu.PrefetchScalarGridSpec(
            num_scalar_prefetch=1,
            grid=(nb, n_pages // pg),
            in_specs=[pl.BlockSpec(memory_space=pltpu.SMEM),
                      seq(t_new), seq(page), seq(page)] + cache + cache,
            out_specs=seq(t_new),
            scratch_shapes=[pltpu.VMEM((rows, width), F32),
                            pltpu.VMEM((rows, 1), F32),
                            pltpu.VMEM((rows, width), F32)]),
        compiler_params=_params("parallel", "arbitrary"),
        name="attn_sample",
    )(page_table, bias, q, k_new, v_new, *([cache_k] * pg), *([cache_v] * pg))


def _ssm_kernel(s_ref, x0r_ref, x0i_ref, ar_ref, ai_ref, bz_ref, cz_ref, d_ref, gw_ref, gb_ref,
                o_ref, xr_ref, xi_ref, bur_sc, bui_sc, *, bs, tt, time_outer):
    step = pl.program_id(0)
    rows = bs * tt
    pitch = rows + 8
    n_slab = ar_ref.shape[0]
    width = s_ref.shape[-1]

    def init():
        xr_ref[...] = x0r_ref[...]
        xi_ref[...] = x0i_ref[...]

    if time_outer:
        pl.when(step == 0)(init)
    else:
        init()

    s = s_ref[...].reshape(rows, width)
    sb = s.astype(BF16)
    per_block = LANES // (2 * SSM_GROUP)
    for c in range(n_slab):
        blk = c // per_block
        bu = _dot(sb[:, blk * LANES:(blk + 1) * LANES], bz_ref[c])
        bur_sc[c * pitch:c * pitch + rows, :] = bu[:, :SLAB]
        bui_sc[c * pitch:c * pitch + rows, :] = bu[:, SLAB:]

    ar = ar_ref[...]
    ai = ai_ref[...]

    def one(row, xr, xi):
        idx = pl.ds(row, n_slab, stride=pitch)
        nr = ar * xr - ai * xi + bur_sc[idx, :]
        ni = ar * xi + ai * xr + bui_sc[idx, :]
        bur_sc[idx, :] = nr
        bui_sc[idx, :] = ni
        return nr, ni

    if time_outer:
        def body(t, st):
            out = []
            for b in range(bs):
                out.append(one(b * tt + t, st[b][0], st[b][1]))
            return tuple(out)

        st = lax.fori_loop(0, tt, body, tuple((xr_ref[b], xi_ref[b]) for b in range(bs)))
        for b in range(bs):
            xr_ref[b] = st[b][0]
            xi_ref[b] = st[b][1]
    else:
        def body(b, _):
            xr, xi = xr_ref[b], xi_ref[b]
            for t in range(tt):
                xr, xi = one(b * tt + t, xr, xi)
            xr_ref[b] = xr
            xi_ref[b] = xi
            return 0

        lax.fori_loop(0, bs, body, 0)

    ys = []
    for blk in range(width // LANES):
        y = jnp.zeros((rows, LANES), F32)
        for c in range(blk * per_block, (blk + 1) * per_block):
            x2 = jnp.concatenate([bur_sc[c * pitch:c * pitch + rows, :],
                                  bui_sc[c * pitch:c * pitch + rows, :]], axis=1).astype(BF16)
            y = y + _dot(x2, cz_ref[c])
        ys.append(y)
    y = jnp.concatenate(ys, axis=1) + d_ref[...] * s
    zg = jax.nn.gelu(y)
    gate = jax.nn.sigmoid(_dot(zg.astype(BF16), gw_ref[...]) + gb_ref[...])
    o_ref[...] = (zg * gate).astype(o_ref.dtype).reshape(o_ref.shape)


def _ssm_tables(lam_re, lam_im, log_dt, b_re, b_im, c_re, c_im):
    g, p = lam_re.shape
    hch = b_re.shape[-1]
    dt = jnp.exp(log_dt)[:, None]
    mag = jnp.exp(lam_re * dt)
    ab_re, ab_im = mag * jnp.cos(lam_im * dt), mag * jnp.sin(lam_im * dt)
    den = lam_re * lam_re + lam_im * lam_im
    nr, ni = ab_re - 1.0, ab_im
    zr, zi = (nr * lam_re + ni * lam_im) / den, (ni * lam_re - nr * lam_im) / den
    bb_re = zr[..., None] * b_re - zi[..., None] * b_im
    bb_im = zr[..., None] * b_im + zi[..., None] * b_re
    n_slab = g * p // SLAB
    gps = SLAB // p
    per_block = LANES // (gps * hch)
    eye_g = jnp.eye(gps, dtype=F32)
    eye_b = jnp.eye(per_block, dtype=F32)

    def in_mat(bb):
        x = bb.reshape(n_slab, gps, p, hch)
        m = jnp.einsum('cepz,ef->cezfp', x, eye_g).reshape(n_slab, gps * hch, SLAB)
        sel = eye_b[jnp.arange(n_slab) % per_block]
        return jnp.einsum('crn,cq->cqrn', m, sel).reshape(n_slab, LANES, SLAB)

    def out_mat(cc):
        x = cc.reshape(n_slab, gps, hch, p)
        m = jnp.einsum('cezp,ef->cepfz', x, eye_g).reshape(n_slab, SLAB, gps * hch)
        sel = eye_b[jnp.arange(n_slab) % per_block]
        return jnp.einsum('cnr,cq->cnqr', m, sel).reshape(n_slab, SLAB, LANES)

    bz = jnp.concatenate([in_mat(bb_re), in_mat(bb_im)], axis=2).astype(BF16)
    cz = jnp.concatenate([out_mat(c_re), -out_mat(c_im)], axis=1).astype(BF16)
    return ab_re.reshape(n_slab, SLAB), ab_im.reshape(n_slab, SLAB), bz, cz


def _ssm(s, x0r, x0i, tables, d_skip, glu_w, glu_b, *, bs, tt, time_outer, out_dtype):
    nseq, length, width = s.shape
    ar, ai, bz, cz = tables
    n_slab = ar.shape[0]
    rows = bs * tt
    pitch = rows + 8
    if time_outer:
        assert bs == nseq
        grid = (length // tt,)
        smap = lambda i: (0, i, 0)
        xmap = lambda i: (0, 0, 0)
    else:
        assert tt == length
        grid = (nseq // bs,)
        smap = lambda i: (i, 0, 0)
        xmap = lambda i: (i, 0, 0)
    const2 = lambda shape: pl.BlockSpec(shape, lambda i: (0, 0))
    const3 = lambda shape: pl.BlockSpec(shape, lambda i: (0, 0, 0))
    xspec = pl.BlockSpec((bs, n_slab, SLAB), xmap)
    return pl.pallas_call(
        functools.partial(_ssm_kernel, bs=bs, tt=tt, time_outer=time_outer),
        out_shape=(jax.ShapeDtypeStruct((nseq, length, width), out_dtype),
                   jax.ShapeDtypeStruct((nseq, n_slab, SLAB), F32),
                   jax.ShapeDtypeStruct((nseq, n_slab, SLAB), F32)),
        grid=grid,
        in_specs=[pl.BlockSpec((bs, tt, width), smap), xspec, xspec,
                  const2((n_slab, SLAB)), const2((n_slab, SLAB)),
                  const3(bz.shape), const3(cz.shape),
                  const2((1, width)), const2(glu_w.shape), const2((1, width))],
        out_specs=(pl.BlockSpec((bs, tt, width), smap), xspec, xspec),
        scratch_shapes=[pltpu.VMEM((n_slab * pitch, SLAB), F32),
                        pltpu.VMEM((n_slab * pitch, SLAB), F32)],
        compiler_params=_params("arbitrary"),
        name="ssm",
    )(s, x0r, x0i, ar, ai, bz, cz, d_skip.reshape(1, width), glu_w, glu_b.reshape(1, width))


def _mix_kernel(oa_ref, ob_ref, ga_ref, gb_ref, h_ref, g2_ref, wa_ref, wb_ref, wo_ref, o_ref):
    merged = (jax.nn.sigmoid(ga_ref[...]) * _dot(oa_ref[...].astype(BF16), wa_ref[...])
              + jax.nn.sigmoid(gb_ref[...]) * _dot(ob_ref[...].astype(BF16), wb_ref[...]))
    mix = _dot(merged.astype(BF16), wo_ref[...])
    o_ref[...] = h_ref[...] + g2_ref[...] * mix


def _mix(oa, ob, ga, gb, h, mod, per_row, rows_per_seq, wa, wb, wo, *, tm):
    t, d = h.shape
    w = oa.shape[1]
    row = lambda width: pl.BlockSpec((tm, width), lambda i: (i, 0))
    full = lambda a: pl.BlockSpec(a.shape, lambda i: (0, 0))
    return pl.pallas_call(
        _mix_kernel,
        out_shape=jax.ShapeDtypeStruct((t, d), F32),
        grid=(t // tm,),
        in_specs=[row(w), row(w), row(d), row(d), row(d),
                  _mod_spec(per_row, tm, rows_per_seq, 5, d),
                  full(wa), full(wb), full(wo)],
        out_specs=row(d),
        compiler_params=_params("parallel"),
        name="mix",
    )(oa, ob, ga, gb, h, mod, wa, wb, wo)


def kernel(x_prompt, x_sample, c_prompt, c_sample, cache_k, cache_v, state_ssm_re, state_ssm_im, page_table, ada_w, ada_b, norm_ffn1, ffn1_w_in, ffn1_w_out, norm_mix, w_in, sb_bias, ssm_lambda_re, ssm_lambda_im, ssm_log_dt, ssm_b_re, ssm_b_im, ssm_c_re, ssm_c_im, ssm_d, glu_w, glu_b, w_branch_a, w_branch_b, w_out, norm_ffn2, ffn2_w_in, ffn2_w_out, final_norm):
    assert ada_w.shape[0] == 1, "single-layer step"
    sq = lambda a: a.reshape(a.shape[1:])
    (ada_w, ada_b, norm_ffn1, ffn1_w_in, ffn1_w_out, norm_mix, w_in, sb_bias, ssm_lambda_re, ssm_lambda_im,
     ssm_log_dt, ssm_b_re, ssm_b_im, ssm_c_re, ssm_c_im, ssm_d, glu_w, glu_b, w_branch_a, w_branch_b, w_out,
     norm_ffn2, ffn2_w_in, ffn2_w_out, cache_k, cache_v, state_ssm_re, state_ssm_im) = map(sq, (
         ada_w, ada_b, norm_ffn1, ffn1_w_in, ffn1_w_out, norm_mix, w_in, sb_bias, ssm_lambda_re, ssm_lambda_im,
         ssm_log_dt, ssm_b_re, ssm_b_im, ssm_c_re, ssm_c_im, ssm_d, glu_w, glu_b, w_branch_a, w_branch_b, w_out,
         norm_ffn2, ffn2_w_in, ffn2_w_out, cache_k, cache_v, state_ssm_re, state_ssm_im))
    bp, sp, d = x_prompt.shape
    bsm, ssm_len, _ = x_sample.shape
    sbw = N_HEADS * HEAD_DIM
    n_pool, page = cache_k.shape[0], cache_k.shape[1]
    n_slab = ssm_lambda_re.shape[0] * ssm_lambda_re.shape[1] // SLAB

    n_c = bp + bsm
    pad_c = (-n_c) % 8
    c_all = jnp.concatenate([c_prompt, c_sample, jnp.zeros((pad_c, d), F32)], axis=0)
    mod = _ada(c_all, ada_w, ada_b)
    mod_p = mod[:bp].reshape(bp, 1, -1)
    mod_s = jnp.repeat(mod[bp:n_c], ssm_len, axis=0)

    bf = lambda a: a.astype(BF16)
    w1i, w1o, w2i, w2o = bf(ffn1_w_in), bf(ffn1_w_out), bf(ffn2_w_in), bf(ffn2_w_out)
    wi, wa, wb, wo, gw = bf(w_in), bf(w_branch_a), bf(w_branch_b), bf(w_out), bf(glu_w)
    row = lambda a: a.reshape(1, -1)
    tables = _ssm_tables(ssm_lambda_re, ssm_lambda_im, ssm_log_dt, ssm_b_re, ssm_b_im, ssm_c_re, ssm_c_im)
    cache_k2 = cache_k.reshape(n_pool, page, sbw)
    cache_v2 = cache_v.reshape(n_pool, page, sbw)

    outs = {}
    for name, x, m, per_row, nseq, length in (("p", x_prompt, mod_p, False, bp, sp),
                                              ("s", x_sample, mod_s, True, bsm, ssm_len)):
        t = nseq * length
        tm = _largest_tile(t, 512, 8)
        tm_in = _largest_tile(t, 256, 8)
        x2 = x.reshape(t, d)
        h1 = _ffn(x2, m, per_row, length, 0, row(norm_ffn1), w1i, w1o, row(final_norm), final=False, tm=tm)
        q, k, v, kb, vb, s, ga, gb = _inproj(h1, m, per_row, length, row(norm_mix), wi, tm=tm_in)
        s3 = s.reshape(nseq, length, -1)
        seq3 = lambda a: a.reshape(nseq, length, sbw)
        if name == "p":
            oa = _attn_prompt(seq3(q), seq3(kb), seq3(vb), sb_bias, tq=min(256, length), pairs=4)
            zero = jnp.zeros((nseq, n_slab, SLAB), F32)
            ob, xr, xi = _ssm(s3, zero, zero, tables, ssm_d, gw, glu_b,
                              bs=nseq, tt=min(64, length), time_outer=True, out_dtype=BF16)
        else:
            oa = _attn_sample(seq3(q).astype(F32), seq3(kb), seq3(vb), cache_k2, cache_v2, page_table, sb_bias)
            ob, xr, xi = _ssm(s3, state_ssm_re.reshape(nseq, n_slab, SLAB),
                              state_ssm_im.reshape(nseq, n_slab, SLAB), tables, ssm_d, gw, glu_b,
                              bs=min(16, nseq), tt=length, time_outer=False, out_dtype=F32)
        h2 = _mix(oa.reshape(t, sbw), ob.reshape(t, -1), ga, gb, h1, m, per_row, length, wa, wb, wo, tm=tm)
        y = _ffn(h2, m, per_row, length, 6, row(norm_ffn2), w2i, w2o, row(final_norm), final=True, tm=tm)
        st_shape = (1, nseq) + ssm_lambda_re.shape
        outs[name] = (y.reshape(nseq, length, d),
                      k.reshape(1, nseq, length, N_HEADS, HEAD_DIM),
                      v.reshape(1, nseq, length, N_HEADS, HEAD_DIM),
                      xr.reshape(st_shape), xi.reshape(st_shape))
    yp, kp, vp, srp, sip = outs["p"]
    ys, ks, vs, srs, sis = outs["s"]
    return (yp, ys, kp, vp, srp, sip, ks, vs, srs, sis)
```
